```python
import math
import jax, jax.numpy as jnp
from jax import lax
import numpy as np

D_MODEL = 2048
BATCH = 4
SEQ = 2048
DEPTH = 1
DEC_BATCH = 8
DEC_SEQ = 8
PAST_LEN = 16384
PAGE_SIZE = 128

N_META = 16
H_A = 8
DH_A = 64
VH_A = 2 * DH_A
W_A = H_A * VH_A
H_B = 4
DK_B = 128
DV_B = 256
W_KB = H_B * DK_B
W_B = H_B * DV_B
GLA_RANK = 16
GLA_TAU = 16.0
GLA_CHUNK = 64
N_EXPERTS = 32
TOP_K = 4
D_EXPERT = 2048
SWIGLU_LIMIT = 7.0
SWIGLU_ALPHA = 1.702
EXPERT_BLOCK = 128
Q_BLOCK = 128
EPS = 1e-6
IN_SPLITS = (W_A, W_A, W_A, W_KB, W_KB, W_B, W_B, GLA_RANK, 2 * D_MODEL)
D_IN = sum(IN_SPLITS)

kernel_name = 'hybrid_diffattn_gla_moe_step'


def rms_norm(x, g):
    xf = x.astype(jnp.float32)
    y = xf * lax.rsqrt(jnp.mean(jnp.square(xf), axis=-1, keepdims=True) + EPS)
    return (y * g.astype(jnp.float32)).astype(x.dtype)


def alibi_slopes():
    h = jnp.arange(1, H_A + 1, dtype=jnp.float32)
    return jnp.exp2(-8.0 * h / H_A)


def diff_attend(q, k, v, q_pos, k_pos, lam):
    f32 = jnp.float32
    s = jnp.einsum('bqhmd,bkhmd->bmhqk', q.astype(f32), k.astype(f32)) * (DH_A ** -0.5)
    dist = jnp.abs(q_pos[:, None] - k_pos[None, :]).astype(f32)
    s = s - alibi_slopes()[:, None, None] * dist
    s = jnp.where(k_pos[None, :] <= q_pos[:, None], s, -jnp.inf)
    p = jax.nn.softmax(s, axis=-1)
    a = p[:, 0] - lam * p[:, 1]
    return jnp.einsum('bhqk,bkhe->bqhe', a, v.astype(f32))


def gla_scan(q, k, v, log_a, s0, chunk):
    B, L = q.shape[:2]
    n = -(-L // chunk)
    pad = n * chunk - L

    def prep(t):
        t = jnp.pad(t, ((0, 0), (0, pad), (0, 0), (0, 0)))
        return jnp.moveaxis(t.reshape(B, n, chunk, *t.shape[2:]), 1, 0)

    qc, kc, vc, ac = prep(q), prep(k), prep(v), prep(log_a)
    tri = jnp.tril(jnp.ones((chunk, chunk), bool))[None, :, :, None, None]

    def step(S, inp):
        qi, ki, vi, ai = inp
        b = jnp.cumsum(ai, axis=1)
        inter = jnp.einsum('bthk,bhkv->bthv', qi * jnp.exp(b), S)
        dec = jnp.exp(jnp.where(tri, b[:, :, None] - b[:, None, :], -jnp.inf))
        att = jnp.sum(qi[:, :, None] * ki[:, None] * dec, axis=-1)
        intra = jnp.einsum('btsh,bshv->bthv', att, vi)
        b_end = b[:, -1]
        S_new = jnp.exp(b_end)[..., None] * S + jnp.einsum('bshk,bshv->bhkv', ki * jnp.exp(b_end[:, None] - b), vi)
        return S_new, inter + intra

    S, o = lax.scan(step, s0.astype(jnp.float32), (qc, kc, vc, ac))
    o = jnp.moveaxis(o, 0, 1).reshape(B, n * chunk, H_B, DV_B)[:, :L]
    return o, S


def mixer_inputs(h, w_in, g_q, g_k, w_fg, b_fg):
    B, L, _ = h.shape
    f32 = jnp.float32
    z = h @ w_in
    offs = np.cumsum(IN_SPLITS)[:-1].tolist()
    qa, ka, va, qb, kb, vb, r, a, gates = jnp.split(z, offs, axis=-1)
    qa = rms_norm(qa.reshape(B, L, H_A, 2, DH_A), g_q)
    ka = rms_norm(ka.reshape(B, L, H_A, 2, DH_A), g_k)
    va = va.reshape(B, L, H_A, VH_A)
    qb = qb.reshape(B, L, H_B, DK_B).astype(f32) * (DK_B ** -0.5)
    kb = kb.reshape(B, L, H_B, DK_B).astype(f32)
    vb = vb.reshape(B, L, H_B, DV_B).astype(f32)
    log_a = jax.nn.log_sigmoid((a @ w_fg + b_fg).astype(f32)).reshape(B, L, H_B, DK_B) / GLA_TAU
    return qa, ka, va, qb, kb, vb, log_a, r, gates


def mixer_output(oa, og, r, gates, lam_init, g_sub, g_gla, w_br_a, w_br_b, w_out, dtype):
    B, L = oa.shape[:2]
    oa = (rms_norm(oa, g_sub) * (1.0 - lam_init)).reshape(B, L, W_A).astype(dtype)
    og = (rms_norm(og, g_gla).reshape(B, L, W_B) * jax.nn.silu(r.astype(jnp.float32))).astype(dtype)
    ga, gb = jnp.split(gates, 2, axis=-1)
    merged = jax.nn.sigmoid(ga) * (oa @ w_br_a) + jax.nn.sigmoid(gb) * (og @ w_br_b)
    return (merged @ w_out).astype(dtype)


def expert_ffn(x, w1, b1, w2, b2):
    u = x @ w1 + b1
    g, lin = jnp.split(u, 2, axis=-1)
    g = jnp.minimum(g, SWIGLU_LIMIT)
    lin = jnp.clip(lin, -SWIGLU_LIMIT, SWIGLU_LIMIT)
    glu = g * jax.nn.sigmoid(SWIGLU_ALPHA * g)
    return ((lin + 1.0) * glu) @ w2 + b2


def moe(h, w_router, b_router, w_e_up, b_e_up, w_e_down, b_e_down):
    N, D = h.shape
    logits = (h @ w_router + b_router).astype(jnp.float32)
    top_val, top_idx = lax.top_k(logits, TOP_K)
    gate = jax.nn.softmax(top_val, axis=-1)
    A = N * TOP_K
    eid = top_idx.reshape(A)
    order = jnp.argsort(eid)
    e_sorted = eid[order]
    tok_sorted = order // TOP_K
    gate_sorted = gate.reshape(A)[order]
    counts = jnp.bincount(eid, length=N_EXPERTS)
    start = jnp.cumsum(counts) - counts
    padded = (counts + EXPERT_BLOCK - 1) // EXPERT_BLOCK * EXPERT_BLOCK
    pend = jnp.cumsum(padded)
    pstart = pend - padded
    row = pstart[e_sorted] + (jnp.arange(A) - start[e_sorted])
    n_blocks = -(-A // EXPERT_BLOCK) + N_EXPERTS
    xbuf = jnp.zeros((n_blocks * EXPERT_BLOCK, D), h.dtype).at[row].set(h[tok_sorted])
    block_start = jnp.arange(n_blocks, dtype=pend.dtype) * EXPERT_BLOCK
    block_e = jnp.minimum(jnp.searchsorted(pend, block_start, side='right'), N_EXPERTS - 1)

    def run(args):
        xb, e = args
        return expert_ffn(xb, w_e_up[e], b_e_up[e], w_e_down[e], b_e_down[e])

    ybuf = lax.map(run, (xbuf.reshape(n_blocks, EXPERT_BLOCK, D), block_e)).reshape(n_blocks * EXPERT_BLOCK, D)
    y_sorted = ybuf[row].astype(jnp.float32) * gate_sorted[:, None]
    y = jnp.zeros((N, D), jnp.float32).at[tok_sorted].add(y_sorted)
    return y.astype(h.dtype)


def ffn_sublayer(x, g_ffn, w_router, b_router, w_e_up, b_e_up, w_e_down, b_e_down):
    B, L, D = x.shape
    h = rms_norm(x, g_ffn).reshape(B * L, D)
    y = moe(h, w_router, b_router, w_e_up, b_e_up, w_e_down, b_e_down)
    return x + y.reshape(B, L, D).astype(x.dtype)


def setup_inputs(seed: int = 0) -> dict:
    key = jax.random.key(seed)
    ks = list(jax.random.split(key, 32))
    f32 = jnp.float32

    def nrm(shape, scale):
        return jax.random.normal(ks.pop(), shape, f32) * scale

    def gain(shape):
        return 1.0 + nrm(shape, 0.02)

    n_pages = PAST_LEN // PAGE_SIZE
    used = DEC_BATCH * n_pages
    n_pool = used + max(1, used // 4)
    page_table = jax.random.permutation(ks.pop(), n_pool)[:used].reshape(DEC_BATCH, n_pages).astype(jnp.int32)
    return {
        'x_prompt': nrm((BATCH, SEQ, D_MODEL), 1.0),
        'x_sample': nrm((DEC_BATCH, DEC_SEQ, D_MODEL), 1.0),
        'cache_k': nrm((DEPTH, n_pool, PAGE_SIZE, H_A, 2 * DH_A), 1.0),
        'cache_v': nrm((DEPTH, n_pool, PAGE_SIZE, H_A, VH_A), 1.0),
        'state_gla': nrm((DEPTH, DEC_BATCH, H_B, DK_B, DV_B), 0.1),
        'page_table': page_table,
        'meta_tokens': nrm((N_META, D_MODEL), 1.0),
        'g_mix': gain((DEPTH, D_MODEL)),
        'w_in': nrm((DEPTH, D_MODEL, D_IN), D_MODEL ** -0.5),
        'g_q': gain((DEPTH, 2, DH_A)),
        'g_k': gain((DEPTH, 2, DH_A)),
        'lam_q1': nrm((DEPTH, DH_A), 0.1),
        'lam_k1': nrm((DEPTH, DH_A), 0.1),
        'lam_q2': nrm((DEPTH, DH_A), 0.1),
        'lam_k2': nrm((DEPTH, DH_A), 0.1),
        'g_sub': gain((DEPTH, VH_A)),
        'w_fg': nrm((DEPTH, GLA_RANK, W_KB), GLA_RANK ** -0.5),
        'b_fg': nrm((DEPTH, W_KB), 0.02),
        'g_gla': gain((DEPTH, DV_B)),
        'w_br_a': nrm((DEPTH, W_A, D_MODEL), W_A ** -0.5),
        'w_br_b': nrm((DEPTH, W_B, D_MODEL), W_B ** -0.5),
        'w_out': nrm((DEPTH, D_MODEL, D_MODEL), D_MODEL ** -0.5),
        'g_ffn': gain((DEPTH, D_MODEL)),
        'w_router': nrm((DEPTH, D_MODEL, N_EXPERTS), D_MODEL ** -0.5),
        'b_router': nrm((DEPTH, N_EXPERTS), 0.01),
        'w_e_up': nrm((DEPTH, N_EXPERTS, D_MODEL, 2 * D_EXPERT), D_MODEL ** -0.5),
        'b_e_up': nrm((DEPTH, N_EXPERTS, 2 * D_EXPERT), 0.02),
        'w_e_down': nrm((DEPTH, N_EXPERTS, D_EXPERT, D_MODEL), D_EXPERT ** -0.5),
        'b_e_down': nrm((DEPTH, N_EXPERTS, D_MODEL), 0.02),
    }


def reference(x_prompt, x_sample, cache_k, cache_v, state_gla, page_table, meta_tokens,
              g_mix, w_in, g_q, g_k, lam_q1, lam_k1, lam_q2, lam_k2, g_sub, w_fg, b_fg, g_gla,
              w_br_a, w_br_b, w_out, g_ffn, w_router, b_router, w_e_up, b_e_up, w_e_down, b_e_down):
    f32 = jnp.float32
    B = x_prompt.shape[0]
    DB, L = x_sample.shape[:2]
    past = page_table.shape[1] * PAGE_SIZE
    meta = jnp.broadcast_to(meta_tokens.astype(x_prompt.dtype)[None], (B, N_META, D_MODEL))
    xp = jnp.concatenate([meta, x_prompt], axis=1)
    T = xp.shape[1]
    xs = x_sample
    nb = -(-T // Q_BLOCK)
    t_pad = nb * Q_BLOCK
    pos_p = jnp.arange(t_pad, dtype=jnp.int32)
    pos_q_s = past + jnp.arange(L, dtype=jnp.int32)
    pos_k_s = jnp.arange(past + L, dtype=jnp.int32)
    kp_new, vp_new, sp_new, ks_new, vs_new, ss_new = [], [], [], [], [], []
    for l in range(DEPTH):
        lam_init = 0.8 - 0.6 * math.exp(-0.3 * l)
        lam = (jnp.exp(jnp.sum(lam_q1[l].astype(f32) * lam_k1[l].astype(f32)))
               - jnp.exp(jnp.sum(lam_q2[l].astype(f32) * lam_k2[l].astype(f32))) + lam_init)

        hp = rms_norm(xp, g_mix[l])
        qa, ka, va, qb, kb, vb, la, r, gates = mixer_inputs(hp, w_in[l], g_q[l], g_k[l], w_fg[l], b_fg[l])
        pw = t_pad - T
        q_pad = jnp.pad(qa, ((0, 0), (0, pw), (0, 0), (0, 0), (0, 0)))
        k_pad = jnp.pad(ka, ((0, 0), (0, pw), (0, 0), (0, 0), (0, 0)))
        v_pad = jnp.pad(va, ((0, 0), (0, pw), (0, 0), (0, 0)))
        q_blocks = jnp.moveaxis(q_pad.reshape(B, nb, Q_BLOCK, H_A, 2, DH_A), 1, 0)
        oa = lax.map(lambda qp: diff_attend(qp[0], k_pad, v_pad, qp[1], pos_p, lam),
                     (q_blocks, pos_p.reshape(nb, Q_BLOCK)))
        oa = jnp.moveaxis(oa, 0, 1).reshape(B, t_pad, H_A, VH_A)[:, :T]
        s0 = jnp.zeros((B, H_B, DK_B, DV_B), f32)
        og_m, s_m = gla_scan(qb[:, :N_META], kb[:, :N_META], vb[:, :N_META], la[:, :N_META], s0, N_META)
        og_r, s_p = gla_scan(qb[:, N_META:], kb[:, N_META:], vb[:, N_META:], la[:, N_META:], s_m, GLA_CHUNK)
        og = jnp.concatenate([og_m, og_r], axis=1)
        xp = xp + mixer_output(oa, og, r, gates, lam_init, g_sub[l], g_gla[l], w_br_a[l], w_br_b[l], w_out[l], xp.dtype)
        xp = ffn_sublayer(xp, g_ffn[l], w_router[l], b_router[l], w_e_up[l], b_e_up[l], w_e_down[l], b_e_down[l])
        kp_new.append(ka.reshape(B, T, H_A, 2 * DH_A))
        vp_new.append(va)
        sp_new.append(s_p)

        hs = rms_norm(xs, g_mix[l])
        qa, ka, va, qb, kb, vb, la, r, gates = mixer_inputs(hs, w_in[l], g_q[l], g_k[l], w_fg[l], b_fg[l])
        k_past = cache_k[l, page_table].reshape(DB, past, H_A, 2, DH_A)
        v_past = cache_v[l, page_table].reshape(DB, past, H_A, VH_A)
        k_all = jnp.concatenate([k_past.astype(f32), ka.astype(f32)], axis=1)
        v_all = jnp.concatenate([v_past.astype(f32), va.astype(f32)], axis=1)
        oa = diff_attend(qa, k_all, v_all, pos_q_s, pos_k_s, lam)
        og, s_s = gla_scan(qb, kb, vb, la, state_gla[l], min(GLA_CHUNK, L))
        xs = xs + mixer_output(oa, og, r, gates, lam_init, g_sub[l], g_gla[l], w_br_a[l], w_br_b[l], w_out[l], xs.dtype)
        xs = ffn_sublayer(xs, g_ffn[l], w_router[l], b_router[l], w_e_up[l], b_e_up[l], w_e_down[l], b_e_down[l])
        ks_new.append(ka.reshape(DB, L, H_A, 2 * DH_A))
        vs_new.append(va)
        ss_new.append(s_s)

    y_prompt = xp[:, N_META:]
    return (y_prompt, xs, jnp.stack(kp_new), jnp.stack(vp_new), jnp.stack(sp_new),
            jnp.stack(ks_new), jnp.stack(vs_new), jnp.stack(ss_new))
```

```python
import functools
import math

import jax
import jax.numpy as jnp
from jax import lax
from jax.experimental import pallas as pl
from jax.experimental.pallas import tpu as pltpu

F32 = jnp.float32
BF16 = jnp.bfloat16
HI = lax.Precision.HIGHEST

N_META = 16
H_A = 8
DH_A = 64
VH_A = 2 * DH_A
W_A = H_A * VH_A
H_B = 4
DK_B = 128
DV_B = 256
W_KB = H_B * DK_B
W_B = H_B * DV_B
GLA_RANK = 16
GLA_TAU = 16.0
GLA_CHUNK = 64
GLA_SUB = 16
N_EXPERTS = 32
TOP_K = 4
SWIGLU_LIMIT = 7.0
SWIGLU_ALPHA = 1.702
PAGE_SIZE = 128
EPS = 1e-6
LAM_INIT = 0.8 - 0.6 * math.exp(-0.3 * 0)

COL_QK = 0
COL_Z2 = 2 * W_A
Z2_COLS = W_A + 2 * W_KB + 2 * W_B
COL_A = COL_Z2 + Z2_COLS
COL_G = COL_A + GLA_RANK

LANES = 128
VMEM_LIMIT = 56 * 1024 * 1024

NEG = -1e30


def _mm(a, b, hi=False):
    if hi:
        return jnp.dot(a, b, precision=HI, preferred_element_type=F32)
    return jnp.dot(a.astype(BF16), b.astype(BF16), preferred_element_type=F32)


def _mm_nt(a, b, hi=False):
    dn = (((1,), (1,)), ((), ()))
    if hi:
        return lax.dot_general(a, b, dn, precision=HI, preferred_element_type=F32)
    return lax.dot_general(a.astype(BF16), b.astype(BF16), dn, preferred_element_type=F32)


def _mm_tn(a, b, hi=False):
    dn = (((0,), (0,)), ((), ()))
    if hi:
        return lax.dot_general(a, b, dn, precision=HI, preferred_element_type=F32)
    return lax.dot_general(a.astype(BF16), b.astype(BF16), dn, preferred_element_type=F32)


def _params(sem, vmem=VMEM_LIMIT):
    return pltpu.CompilerParams(dimension_semantics=sem, vmem_limit_bytes=vmem)


def _rms_kernel(x_ref, g_ref, o_ref):
    x = x_ref[...]
    ms = jnp.mean(x * x, axis=-1, keepdims=True)
    o_ref[...] = (x * lax.rsqrt(ms + EPS) * g_ref[...]).astype(o_ref.dtype)


def _rms(x, g, out_dtype, tm):
    m, d = x.shape
    return pl.pallas_call(
        _rms_kernel,
        grid=(m // tm,),
        in_specs=[pl.BlockSpec((tm, d), lambda i: (i, 0)),
                  pl.BlockSpec((1, d), lambda i: (0, 0))],
        out_specs=pl.BlockSpec((tm, d), lambda i: (i, 0)),
        out_shape=jax.ShapeDtypeStruct((m, d), out_dtype),
        compiler_params=_params(("arbitrary",)),
        name="rms_rows",
    )(x, g.reshape(1, d))


def _proj_kernel(*refs, mode, hi, tn):
    h_ref, w_ref = refs[0], refs[1]
    if mode == "qk":
        g_ref, o_ref = refs[2], refs[3]
        scr = refs[4:]
    elif mode == "loga":
        wfg_ref, bfg_ref, o_ref = refs[2], refs[3], refs[4]
        scr = refs[5:]
    else:
        o_ref = refs[2]
        scr = refs[3:]

    if hi:
        z = jnp.dot(h_ref[...], w_ref[...], precision=HI, preferred_element_type=F32)
    else:
        wbf = scr[0]

        @pl.when(pl.program_id(1) == 0)
        def _cast():
            wbf[...] = w_ref[...].astype(BF16)

        z = jnp.dot(h_ref[...], wbf[...], preferred_element_type=F32)

    if mode == "plain":
        o_ref[...] = z
    elif mode == "gate":
        o_ref[...] = jax.nn.sigmoid(z)
    elif mode == "qk":
        lane = lax.broadcasted_iota(jnp.int32, (z.shape[0], LANES), 1)
        lo = lane < DH_A
        g = g_ref[0]
        for c in range(tn // LANES):
            x = z[:, c * LANES:(c + 1) * LANES]
            x2 = x * x
            s0 = jnp.sum(jnp.where(lo, x2, 0.0), axis=-1, keepdims=True)
            s1 = jnp.sum(jnp.where(lo, 0.0, x2), axis=-1, keepdims=True)
            inv = jnp.where(lo, lax.rsqrt(s0 * (1.0 / DH_A) + EPS),
                            lax.rsqrt(s1 * (1.0 / DH_A) + EPS))
            o_ref[:, c * LANES:(c + 1) * LANES] = x * inv * g
    else:
        lane = lax.broadcasted_iota(jnp.int32, z.shape, 1)
        a = jnp.where(lane < GLA_RANK, z, 0.0)
        u = _mm(a, wfg_ref[...], hi) + bfg_ref[...]
        o_ref[...] = -(jnp.maximum(-u, 0.0) + jnp.log1p(jnp.exp(-jnp.abs(u)))) * (1.0 / GLA_TAU)


def _proj(h, w, col0, n_cols, tn, tm, mode, hi, extras=(), name="proj"):
    m, k = h.shape
    assert col0 % tn == 0 and n_cols % tn == 0 and m % tm == 0
    cb0 = col0 // tn
    in_specs = [pl.BlockSpec((tm, k), lambda j, i: (i, 0)),
                pl.BlockSpec((k, tn), lambda j, i: (0, cb0 + j))]
    out_cols, out_tn = n_cols, tn
    if mode == "qk":
        in_specs.append(pl.BlockSpec((1, 1, LANES), lambda j, i: ((j * tn) // W_A, 0, 0)))
    elif mode == "loga":
        in_specs += [pl.BlockSpec((LANES, W_KB), lambda j, i: (0, 0)),
                     pl.BlockSpec((1, W_KB), lambda j, i: (0, 0))]
        out_cols, out_tn = W_KB, W_KB
    return pl.pallas_call(
        functools.partial(_proj_kernel, mode=mode, hi=hi, tn=tn),
        grid=(n_cols // tn, m // tm),
        in_specs=in_specs,
        out_specs=pl.BlockSpec((tm, out_tn), lambda j, i: (i, j)),
        out_shape=jax.ShapeDtypeStruct((m, out_cols), F32),
        scratch_shapes=[] if hi else [pltpu.VMEM((k, tn), BF16)],
        compiler_params=_params(("arbitrary", "arbitrary")),
        name=name,
    )(h, w, *extras)


def _attn_kernel(sc_ref, q_ref, k_ref, v_ref, km_ref, vm_ref, gs_ref, o_ref, kbf, vbf, *, tq, tk):
    h = pl.program_id(1)
    i = pl.program_id(2)

    @pl.when(i == 0)
    def _cast():
        kbf[...] = k_ref[...].astype(BF16)
        vbf[...] = v_ref[...].astype(BF16)

    lam = sc_ref[0]
    slope = sc_ref[1 + h]
    q = q_ref[...] * (DH_A ** -0.5)
    lane = lax.broadcasted_iota(jnp.int32, (tq, LANES), 1)
    lo = lane < DH_A
    q12 = jnp.concatenate([jnp.where(lo, q, 0.0), jnp.where(lo, 0.0, q)], axis=0).astype(BF16)

    q0 = i * tq
    rr = lax.broadcasted_iota(jnp.int32, (2 * tq, tk), 0) % tq
    cc = lax.broadcasted_iota(jnp.int32, (2 * tq, tk), 1)
    relb = (rr - cc).astype(F32) * (-slope)

    rm = lax.broadcasted_iota(jnp.int32, (2 * tq, N_META), 0) % tq
    cm = lax.broadcasted_iota(jnp.int32, (2 * tq, N_META), 1)
    s = _mm_nt(q12, km_ref[...])
    s = s + (rm - cm + (q0 + N_META)).astype(F32) * (-slope)
    m0 = jnp.max(s, axis=-1, keepdims=True)
    p = jnp.exp(s - m0)
    l0 = jnp.sum(p, axis=-1, keepdims=True)
    acc0 = _mm(p, vm_ref[...])

    def chunk(j, carry, masked):
        m, l, acc = carry
        k0 = pl.multiple_of(j * tk, tk)
        kc = kbf[pl.ds(k0, tk), :]
        vc = vbf[pl.ds(k0, tk), :]
        s = lax.dot_general(q12, kc, (((1,), (1,)), ((), ())), preferred_element_type=F32)
        s = s + relb + (q0 - k0).astype(F32) * (-slope)
        if masked:
            s = jnp.where(rr + q0 >= cc + k0, s, NEG)
        m_new = jnp.maximum(m, jnp.max(s, axis=-1, keepdims=True))
        alpha = jnp.exp(m - m_new)
        p = jnp.exp(s - m_new)
        l = alpha * l + jnp.sum(p, axis=-1, keepdims=True)
        acc = alpha * acc + jnp.dot(p.astype(BF16), vc, preferred_element_type=F32)
        return m_new, l, acc

    n_full = q0 // tk
    carry = lax.fori_loop(0, n_full, lambda j, c: chunk(j, c, False), (m0, l0, acc0))
    _, l, acc = chunk(n_full, carry, True)

    o = acc[:tq] / l[:tq] - lam * (acc[tq:] / l[tq:])
    ms = jnp.mean(o * o, axis=-1, keepdims=True)
    y = o * lax.rsqrt(ms + EPS) * gs_ref[...] * (1.0 - LAM_INIT)
    o_ref[...] = y.astype(o_ref.dtype)


def _prompt_attention(sc, qk, z2, g_sub, n_batch, seq, meta_row0, tq=128, tk=256):
    nq = seq // tq
    n_prompt = n_batch * seq
    meta_blk = meta_row0 // N_META
    kcol0 = W_A // LANES
    return pl.pallas_call(
        functools.partial(_attn_kernel, tq=tq, tk=tk),
        grid=(n_batch, H_A, nq),
        in_specs=[
            pl.BlockSpec(memory_space=pltpu.SMEM),
            pl.BlockSpec((tq, LANES), lambda b, h, i: (b * nq + i, h)),
            pl.BlockSpec((seq, LANES), lambda b, h, i: (b, kcol0 + h)),
            pl.BlockSpec((seq, LANES), lambda b, h, i: (b, h)),
            pl.BlockSpec((N_META, LANES), lambda b, h, i: (meta_blk, kcol0 + h)),
            pl.BlockSpec((N_META, LANES), lambda b, h, i: (meta_blk, h)),
            pl.BlockSpec((1, LANES), lambda b, h, i: (0, 0)),
        ],
        out_specs=pl.BlockSpec((tq, LANES), lambda b, h, i: (b * nq + i, h)),
        out_shape=jax.ShapeDtypeStruct((n_prompt, W_A), BF16),
        scratch_shapes=[pltpu.VMEM((seq, LANES), BF16), pltpu.VMEM((seq, LANES), BF16)],
        compiler_params=_params(("arbitrary", "arbitrary", "arbitrary")),
        name="prompt_attention",
    )(sc, qk, qk, z2, qk, z2, g_sub.reshape(1, VH_A))


def _dec_kernel(pt_ref, sc_ref, qbd_ref, kn_ref, vn_ref, slc_ref, tcol_ref, gs_ref, *rest,
                g_pages, n_steps, past, n_new):
    kp_refs = rest[:g_pages]
    vp_refs = rest[g_pages:2 * g_pages]
    o_ref = rest[2 * g_pages]
    m_s, l_s, acc_s = rest[2 * g_pages + 1:]
    j = pl.program_id(1)

    @pl.when(j == 0)
    def _init():
        m_s[...] = jnp.full(m_s.shape, NEG, F32)
        l_s[...] = jnp.zeros(l_s.shape, F32)
        acc_s[...] = jnp.zeros(acc_s.shape, F32)

    qbd = qbd_ref[...]
    slc = slc_ref[...]
    tcol = tcol_ref[...]
    keyrow = lax.broadcasted_iota(jnp.int32, (PAGE_SIZE, LANES), 0).astype(F32)
    rel = tcol - keyrow

    def update(s_all, pv_fn):
        m_old = m_s[...]
        m_new = jnp.maximum(m_old, jnp.max(s_all, axis=0, keepdims=True))
        alpha = jnp.exp(m_old - m_new)
        p = jnp.exp(s_all - m_new)
        l_s[...] = alpha * l_s[...] + jnp.sum(p, axis=0, keepdims=True)
        acc_s[...] = acc_s[...] * alpha + pv_fn(p)
        m_s[...] = m_new

    ss = []
    for u in range(g_pages):
        page0 = (past - (j * g_pages + u) * PAGE_SIZE).astype(F32)
        s = jnp.dot(kp_refs[u][...], qbd, precision=HI, preferred_element_type=F32)
        ss.append(s + (rel + page0) * (-slc))
    s_all = jnp.concatenate(ss, axis=0)

    def pv_pages(p):
        out = None
        for u in range(g_pages):
            t = _mm_tn(vp_refs[u][...], p[u * PAGE_SIZE:(u + 1) * PAGE_SIZE], True)
            out = t if out is None else out + t
        return out

    update(s_all, pv_pages)

    @pl.when(j == n_steps - 1)
    def _finish():
        trow = lax.broadcasted_iota(jnp.int32, (n_new, LANES), 0).astype(F32)
        s = jnp.dot(kn_ref[...], qbd, precision=HI, preferred_element_type=F32)
        s = s + (tcol - trow) * (-slc)
        s = jnp.where(trow <= tcol, s, NEG)
        update(s, lambda p: _mm_tn(vn_ref[...], p, True))

        lam = sc_ref[0]
        accn = acc_s[...] / l_s[...]
        for h in range(H_A):
            blk = accn[h * VH_A:(h + 1) * VH_A, h * 2 * n_new:(h + 1) * 2 * n_new]
            d = blk[:, :n_new] - lam * blk[:, n_new:]
            ms = jnp.mean(d * d, axis=0, keepdims=True)
            o_ref[h * VH_A:(h + 1) * VH_A, :] = d * lax.rsqrt(ms + EPS) * gs_ref[...] * (1.0 - LAM_INIT)


def _decode_attention(page_table, sc, qbd, qk_s, z2_s, cache_k, cache_v, slc, tcol, g_sub, g_pages=4):
    n_db, n_pages = page_table.shape
    n_new = qk_s.shape[0] // n_db
    past = n_pages * PAGE_SIZE
    n_steps = n_pages // g_pages
    n_pool = cache_k.shape[0]
    width = H_A * VH_A
    ck = cache_k.reshape(n_pool, PAGE_SIZE, width)
    cv = cache_v.reshape(n_pool, PAGE_SIZE, width)
    kcol0 = W_A // width

    def page_spec(u):
        return pl.BlockSpec((None, PAGE_SIZE, width),
                            lambda d, j, pt: (pt[d * n_pages + j * g_pages + u], 0, 0))

    in_specs = [
        pl.BlockSpec(memory_space=pltpu.SMEM),
        pl.BlockSpec((None, width, LANES), lambda d, j, pt: (d, 0, 0)),
        pl.BlockSpec((n_new, width), lambda d, j, pt: (d, kcol0)),
        pl.BlockSpec((n_new, width), lambda d, j, pt: (d, 0)),
        pl.BlockSpec((1, LANES), lambda d, j, pt: (0, 0)),
        pl.BlockSpec((1, LANES), lambda d, j, pt: (0, 0)),
        pl.BlockSpec((VH_A, 1), lambda d, j, pt: (0, 0)),
    ] + [page_spec(u) for u in range(g_pages)] * 2
    grid_spec = pltpu.PrefetchScalarGridSpec(
        num_scalar_prefetch=1,
        grid=(n_db, n_steps),
        in_specs=in_specs,
        out_specs=pl.BlockSpec((None, width, n_new), lambda d, j, pt: (d, 0, 0)),
        scratch_shapes=[pltpu.VMEM((1, LANES), F32), pltpu.VMEM((1, LANES), F32),
                        pltpu.VMEM((width, LANES), F32)],
    )
    return pl.pallas_call(
        functools.partial(_dec_kernel, g_pages=g_pages, n_steps=n_steps, past=past, n_new=n_new),
        grid_spec=grid_spec,
        out_shape=jax.ShapeDtypeStruct((n_db, width, n_new), F32),
        compiler_params=_params(("arbitrary", "arbitrary")),
        name="decode_attention",
    )(page_table.reshape(-1), sc, qbd, qk_s, z2_s, slc, tcol, g_sub.reshape(VH_A, 1),
      *([ck] * g_pages), *([cv] * g_pages))


def _gla_kernel(q_ref, k_ref, v_ref, la_ref, r_ref, s0_ref, gg_ref, o_ref, so_ref, state, *,
                chunk, sub, n_chunks, hi):
    state[...] = s0_ref[...]
    row = lax.broadcasted_iota(jnp.int32, (chunk, chunk), 0)
    col = lax.broadcasted_iota(jnp.int32, (chunk, chunk), 1)
    tri = (col <= row).astype(F32)
    tri_sub = (col <= (row // sub) * sub + (sub - 1)).astype(F32)
    causal = col <= row
    colblk = col // sub
    rowc = lax.broadcasted_iota(jnp.int32, (chunk, DK_B), 0)
    ones = jnp.ones((chunk, DK_B), F32)
    n_sub = chunk // sub

    def body(c, carry):
        c0 = pl.multiple_of(c * chunk, chunk)
        sl = pl.ds(c0, chunk)
        la = la_ref[sl, :]
        b = jnp.dot(tri, la, precision=HI, preferred_element_type=F32)
        rsub = jnp.dot(tri_sub, la, precision=HI, preferred_element_type=F32)
        b_end = b[chunk - 1:chunk, :]
        qs = q_ref[sl, :] * (DK_B ** -0.5)
        kk = k_ref[sl, :]
        vv = v_ref[sl, :]
        s_old = state[...]

        inter = _mm(qs * jnp.exp(b), s_old, hi)

        kt = kk * jnp.exp(rsub - b)
        qj = []
        for jb in range(n_sub):
            ref = rsub[jb * sub:jb * sub + 1, :]
            qj.append(qs * jnp.exp(jnp.where(rowc >= jb * sub, b - ref, -jnp.inf)))
        a4 = _mm_nt(jnp.concatenate(qj, axis=0), kt, hi)
        att = a4[:chunk]
        for jb in range(1, n_sub):
            att = jnp.where(colblk == jb, a4[jb * chunk:(jb + 1) * chunk], att)
        att = jnp.where(causal, att, 0.0)
        o = inter + _mm(att, vv, hi)

        dec = jnp.exp(_mm_tn(la, ones, True))
        kd = kk * jnp.exp(b_end - b)
        state[...] = jnp.concatenate([dec] * (DV_B // DK_B), axis=1) * s_old + _mm_tn(kd, vv, hi)

        ms = jnp.mean(o * o, axis=-1, keepdims=True)
        r = r_ref[sl, :]
        y = o * lax.rsqrt(ms + EPS) * gg_ref[...] * (r * jax.nn.sigmoid(r))
        o_ref[sl, :] = y.astype(o_ref.dtype)
        return carry

    lax.fori_loop(0, n_chunks, body, 0)
    so_ref[...] = state[...]


def _gla(z2, la, s0, g_gla, n_batch, seq, row0, chunk, sub, hi, out_dtype, name):
    rb0 = row0 // seq
    qc0 = W_A // DK_B
    kc0 = (W_A + W_KB) // DK_B
    vc0 = (W_A + 2 * W_KB) // DV_B
    rc0 = (W_A + 2 * W_KB + W_B) // DV_B
    per_batch_state = s0.shape[0] != 1
    return pl.pallas_call(
        functools.partial(_gla_kernel, chunk=chunk, sub=sub, n_chunks=seq // chunk, hi=hi),
        grid=(n_batch, H_B),
        in_specs=[
            pl.BlockSpec((seq, DK_B), lambda b, h: (rb0 + b, qc0 + h)),
            pl.BlockSpec((seq, DK_B), lambda b, h: (rb0 + b, kc0 + h)),
            pl.BlockSpec((seq, DV_B), lambda b, h: (rb0 + b, vc0 + h)),
            pl.BlockSpec((seq, DK_B), lambda b, h: (rb0 + b, h)),
            pl.BlockSpec((seq, DV_B), lambda b, h: (rb0 + b, rc0 + h)),
            pl.BlockSpec((None, None, DK_B, DV_B),
                         (lambda b, h: (b, h, 0, 0)) if per_batch_state else (lambda b, h: (0, h, 0, 0))),
            pl.BlockSpec((1, DV_B), lambda b, h: (0, 0)),
        ],
        out_specs=[
            pl.BlockSpec((seq, DV_B), lambda b, h: (b, h)),
            pl.BlockSpec((None, None, DK_B, DV_B), lambda b, h: (b, h, 0, 0)),
        ],
        out_shape=[jax.ShapeDtypeStruct((n_batch * seq, W_B), out_dtype),
                   jax.ShapeDtypeStruct((n_batch, H_B, DK_B, DV_B), F32)],
        scratch_shapes=[pltpu.VMEM((DK_B, DV_B), F32)],
        compiler_params=_params(("arbitrary", "arbitrary")),
        name=name,
    )(z2, z2, z2, la, z2, s0, g_gla.reshape(1, DV_B))


def _merge_kernel(oa_ref, og_ref, h_ref, wa_ref, wb_ref, wga_ref, wgb_ref, o_ref):
    h = h_ref[...]
    pa = jnp.dot(oa_ref[...], wa_ref[...], preferred_element_type=F32)
    ga = jax.nn.sigmoid(jnp.dot(h, wga_ref[...], preferred_element_type=F32))
    acc = ga * pa
    pb = jnp.dot(og_ref[...], wb_ref[...], preferred_element_type=F32)
    gb = jax.nn.sigmoid(jnp.dot(h, wgb_ref[...], preferred_element_type=F32))
    o_ref[...] = (acc + gb * pb).astype(o_ref.dtype)


def _merge(oa, og, h, wa, wb, wg, tm, tn):
    m, k = h.shape
    d = wa.shape[1]
    gb0 = d // tn
    return pl.pallas_call(
        _merge_kernel,
        grid=(d // tn, m // tm),
        in_specs=[
            pl.BlockSpec((tm, W_A), lambda j, i: (i, 0)),
            pl.BlockSpec((tm, W_B), lambda j, i: (i, 0)),
            pl.BlockSpec((tm, k), lambda j, i: (i, 0)),
            pl.BlockSpec((W_A, tn), lambda j, i: (0, j)),
            pl.BlockSpec((W_B, tn), lambda j, i: (0, j)),
            pl.BlockSpec((k, tn), lambda j, i: (0, j)),
            pl.BlockSpec((k, tn), lambda j, i: (0, gb0 + j)),
        ],
        out_specs=pl.BlockSpec((tm, tn), lambda j, i: (i, j)),
        out_shape=jax.ShapeDtypeStruct((m, d), BF16),
        compiler_params=_params(("arbitrary", "arbitrary")),
        name="merge",
    )(oa, og, h, wa, wb, wg, wg)


def _merge_s_kernel(oa_ref, og_ref, wa_ref, wb_ref, ga_ref, gb_ref, o_ref):
    pa = _mm(oa_ref[...], wa_ref[...], True)
    pb = _mm(og_ref[...], wb_ref[...], True)
    o_ref[...] = ga_ref[...] * pa + gb_ref[...] * pb


def _merge_sample(oa, og, wa, wb, sg, tn):
    m = oa.shape[0]
    d = wa.shape[1]
    gb0 = d // tn
    return pl.pallas_call(
        _merge_s_kernel,
        grid=(d // tn,),
        in_specs=[
            pl.BlockSpec((m, W_A), lambda j: (0, 0)),
            pl.BlockSpec((m, W_B), lambda j: (0, 0)),
            pl.BlockSpec((W_A, tn), lambda j: (0, j)),
            pl.BlockSpec((W_B, tn), lambda j: (0, j)),
            pl.BlockSpec((m, tn), lambda j: (0, j)),
            pl.BlockSpec((m, tn), lambda j: (0, gb0 + j)),
        ],
        out_specs=pl.BlockSpec((m, tn), lambda j: (0, j)),
        out_shape=jax.ShapeDtypeStruct((m, d), F32),
        compiler_params=_params(("arbitrary",)),
        name="merge_sample",
    )(oa, og, wa, wb, sg, sg)


def _outproj_kernel(mg_ref, w_ref, x_ref, gf_ref, x1_ref, h2_ref, *, hi):
    x1 = x_ref[...] + _mm(mg_ref[...], w_ref[...], hi)
    x1_ref[...] = x1
    ms = jnp.mean(x1 * x1, axis=-1, keepdims=True)
    h2_ref[...] = (x1 * lax.rsqrt(ms + EPS) * gf_ref[...]).astype(h2_ref.dtype)


def _outproj(mg, w, x, g_ffn, tm, hi, name):
    m, d = x.shape
    return pl.pallas_call(
        functools.partial(_outproj_kernel, hi=hi),
        grid=(m // tm,),
        in_specs=[
            pl.BlockSpec((tm, d), lambda i: (i, 0)),
            pl.BlockSpec((d, d), lambda i: (0, 0)),
            pl.BlockSpec((tm, d), lambda i: (i, 0)),
            pl.BlockSpec((1, d), lambda i: (0, 0)),
        ],
        out_specs=[pl.BlockSpec((tm, d), lambda i: (i, 0)), pl.BlockSpec((tm, d), lambda i: (i, 0))],
        out_shape=[jax.ShapeDtypeStruct((m, d), F32), jax.ShapeDtypeStruct((m, d), F32 if hi else BF16)],
        compiler_params=_params(("arbitrary",)),
        name=name,
    )(mg, w, x, g_ffn.reshape(1, d))


def _route_kernel(h_ref, w_ref, b_ref, c0_ref, idx_ref, gate_ref, rank_ref, cnt_ref, carry, *, tm, hi):
    @pl.when(pl.program_id(0) == 0)
    def _init():
        carry[...] = c0_ref[...]

    logits = _mm(h_ref[...], w_ref[...], hi) + b_ref[...]
    lane = lax.broadcasted_iota(jnp.int32, (tm, N_EXPERTS), 1).astype(F32)
    work = logits
    vals, idxs, sels = [], [], []
    for _ in range(TOP_K):
        mx = jnp.max(work, axis=-1, keepdims=True)
        ix = jnp.min(jnp.where(work == mx, lane, float(N_EXPERTS)), axis=-1, keepdims=True)
        sel = lane == ix
        vals.append(mx)
        idxs.append(ix)
        sels.append(sel)
        work = jnp.where(sel, -jnp.inf, work)
    es = [jnp.exp(v - vals[0]) for v in vals]
    tot = es[0] + es[1] + es[2] + es[3]
    gate_ref[...] = jnp.concatenate([e / tot for e in es], axis=1)
    idx_ref[...] = jnp.concatenate(idxs, axis=1).astype(jnp.int32)

    onehot = jnp.zeros((tm, N_EXPERTS), F32)
    for sel in sels:
        onehot = onehot + sel.astype(F32)
    r = lax.broadcasted_iota(jnp.int32, (tm, tm), 0)
    c = lax.broadcasted_iota(jnp.int32, (tm, tm), 1)
    before = (c < r).astype(BF16)
    cum = jnp.dot(before, onehot.astype(BF16), preferred_element_type=F32) + carry[...]
    ranks = [jnp.sum(jnp.where(sel, cum, 0.0), axis=-1, keepdims=True) for sel in sels]
    rank_ref[...] = jnp.concatenate(ranks, axis=1).astype(jnp.int32)
    carry[...] = carry[...] + jnp.sum(onehot, axis=0, keepdims=True)
    cnt_ref[...] = carry[...]


def _route(h, n_tok, w_router, b_router, c0, tm, hi, name):
    d = h.shape[1]
    assert n_tok % tm == 0
    return pl.pallas_call(
        functools.partial(_route_kernel, tm=tm, hi=hi),
        grid=(n_tok // tm,),
        in_specs=[
            pl.BlockSpec((tm, d), lambda i: (i, 0)),
            pl.BlockSpec((d, N_EXPERTS), lambda i: (0, 0)),
            pl.BlockSpec((1, N_EXPERTS), lambda i: (0, 0)),
            pl.BlockSpec((1, N_EXPERTS), lambda i: (0, 0)),
        ],
        out_specs=[
            pl.BlockSpec((tm, TOP_K), lambda i: (i, 0)),
            pl.BlockSpec((tm, TOP_K), lambda i: (i, 0)),
            pl.BlockSpec((tm, TOP_K), lambda i: (i, 0)),
            pl.BlockSpec((1, N_EXPERTS), lambda i: (0, 0)),
        ],
        out_shape=[
            jax.ShapeDtypeStruct((n_tok, TOP_K), jnp.int32),
            jax.ShapeDtypeStruct((n_tok, TOP_K), F32),
            jax.ShapeDtypeStruct((n_tok, TOP_K), jnp.int32),
            jax.ShapeDtypeStruct((1, N_EXPERTS), F32),
        ],
        scratch_shapes=[pltpu.VMEM((1, N_EXPERTS), F32)],
        compiler_params=_params(("arbitrary",)),
        name=name,
    )(h, w_router, b_router.reshape(1, N_EXPERTS), c0)


def _up_kernel(be_ref, first_ref, nu_ref, x_ref, wg_ref, wl_ref, bg_ref, bl_ref, o_ref, wgb, wlb):
    b = pl.program_id(1)

    @pl.when(first_ref[b] == 1)
    def _cast():
        wgb[...] = wg_ref[...].astype(BF16)
        wlb[...] = wl_ref[...].astype(BF16)

    @pl.when(b < nu_ref[0])
    def _compute():
        x = x_ref[...]
        g = jnp.dot(x, wgb[...], preferred_element_type=F32) + bg_ref[...]
        lin = jnp.dot(x, wlb[...], preferred_element_type=F32) + bl_ref[...]
        g = jnp.minimum(g, SWIGLU_LIMIT)
        lin = jnp.clip(lin, -SWIGLU_LIMIT, SWIGLU_LIMIT)
        glu = g * jax.nn.sigmoid(SWIGLU_ALPHA * g)
        o_ref[...] = ((lin + 1.0) * glu).astype(o_ref.dtype)


def _expert_up(be, first, nu, xbuf, w_up, b_up, tm, th):
    rows, d = xbuf.shape
    d_e = w_up.shape[2] // 2
    nj = d_e // th
    grid_spec = pltpu.PrefetchScalarGridSpec(
        num_scalar_prefetch=3,
        grid=(nj, rows // tm),
        in_specs=[
            pl.BlockSpec((tm, d), lambda j, b, be, fi, nu: (b, 0)),
            pl.BlockSpec((None, d, th), lambda j, b, be, fi, nu: (be[b], 0, j)),
            pl.BlockSpec((None, d, th), lambda j, b, be, fi, nu: (be[b], 0, nj + j)),
            pl.BlockSpec((None, 1, th), lambda j, b, be, fi, nu: (be[b], 0, j)),
            pl.BlockSpec((None, 1, th), lambda j, b, be, fi, nu: (be[b], 0, nj + j)),
        ],
        out_specs=pl.BlockSpec((tm, th), lambda j, b, be, fi, nu: (b, j)),
        scratch_shapes=[pltpu.VMEM((d, th), BF16), pltpu.VMEM((d, th), BF16)],
    )
    return pl.pallas_call(
        _up_kernel,
        grid_spec=grid_spec,
        out_shape=jax.ShapeDtypeStruct((rows, d_e), BF16),
        compiler_params=_params(("arbitrary", "arbitrary")),
        name="expert_up",
    )(be, first, nu, xbuf, w_up, w_up, b_up, b_up)


def _down_kernel(be_ref, first_ref, nu_ref, a_ref, w_ref, b_ref, o_ref, wb):
    b = pl.program_id(1)

    @pl.when(first_ref[b] == 1)
    def _cast():
        wb[...] = w_ref[...].astype(BF16)

    @pl.when(b < nu_ref[0])
    def _compute():
        o_ref[...] = jnp.dot(a_ref[...], wb[...], preferred_element_type=F32) + b_ref[...]


def _expert_down(be, first, nu, act, w_down, b_down, tm, tn):
    rows, d_e = act.shape
    d = w_down.shape[2]
    grid_spec = pltpu.PrefetchScalarGridSpec(
        num_scalar_prefetch=3,
        grid=(d // tn, rows // tm),
        in_specs=[
            pl.BlockSpec((tm, d_e), lambda j, b, be, fi, nu: (b, 0)),
            pl.BlockSpec((None, d_e, tn), lambda j, b, be, fi, nu: (be[b], 0, j)),
            pl.BlockSpec((None, 1, tn), lambda j, b, be, fi, nu: (be[b], 0, j)),
        ],
        out_specs=pl.BlockSpec((tm, tn), lambda j, b, be, fi, nu: (b, j)),
        scratch_shapes=[pltpu.VMEM((d_e, tn), BF16)],
    )
    return pl.pallas_call(
        _down_kernel,
        grid_spec=grid_spec,
        out_shape=jax.ShapeDtypeStruct((rows, d), F32),
        compiler_params=_params(("arbitrary", "arbitrary")),
        name="expert_down",
    )(be, first, nu, act, w_down, b_down)


def kernel(x_prompt, x_sample, cache_k, cache_v, state_gla, page_table, meta_tokens, g_mix, w_in, g_q, g_k, lam_q1, lam_k1, lam_q2, lam_k2, g_sub, w_fg, b_fg, g_gla, w_br_a, w_br_b, w_out, g_ffn, w_router, b_router, w_e_up, b_e_up, w_e_down, b_e_down):
    assert w_in.shape[0] == 1, "single-layer trunk"
    n_batch, seq, d = x_prompt.shape
    n_db, n_new, _ = x_sample.shape
    assert 2 * H_A * n_new == LANES
    n_prompt = n_batch * seq
    n_sample = n_db * n_new
    n_tok = n_prompt + n_sample
    row_m = n_prompt

    w = w_in[0]
    lam = (jnp.exp(jnp.sum(lam_q1[0] * lam_k1[0])) - jnp.exp(jnp.sum(lam_q2[0] * lam_k2[0])) + LAM_INIT)
    slopes = jnp.exp2(-8.0 * jnp.arange(1, H_A + 1, dtype=F32) / H_A)
    sc = jnp.concatenate([lam.reshape(1), slopes]).astype(F32)
    g_qk = jnp.stack([g_q[0].reshape(1, VH_A), g_k[0].reshape(1, VH_A)])
    wfg_pad = jnp.zeros((LANES, W_KB), F32).at[:GLA_RANK].set(w_fg[0])
    bfg = b_fg[0].reshape(1, W_KB)
    w_gates = w[:, COL_G:]

    pad_rows = (-(n_prompt + N_META)) % (5 * 13 * LANES)
    xm = jnp.concatenate([x_prompt.reshape(n_prompt, d), meta_tokens.astype(F32),
                          jnp.zeros((pad_rows, d), F32)], axis=0)
    n_rows = xm.shape[0]
    tm_proj = n_rows // 5
    tm_row = n_rows // 13
    hm = _rms(xm, g_mix[0], BF16, tm=tm_row)
    qk = _proj(hm, w, COL_QK, 2 * W_A, 512, tm_proj, "qk", False, (g_qk,), name="proj_qk")
    z2 = _proj(hm, w, COL_Z2, Z2_COLS, 512, tm_proj, "plain", False, name="proj_z2")
    la = _proj(hm, w, COL_A, LANES, LANES, tm_proj, "loga", False, (wfg_pad, bfg), name="proj_loga")

    oa_p = _prompt_attention(sc, qk, z2, g_sub[0], n_batch, seq, row_m)
    zero_state = jnp.zeros((1, H_B, DK_B, DV_B), F32)
    _, s_meta = _gla(z2, la, zero_state, g_gla[0], 1, N_META, row_m, N_META, N_META, False, BF16, "gla_meta")
    og_p, s_prompt = _gla(z2, la, s_meta, g_gla[0], n_batch, seq, 0, GLA_CHUNK, GLA_SUB, False, BF16,
                          "gla_prompt")
    tail = jnp.zeros((n_rows - n_prompt, W_A), BF16)
    mg = _merge(jnp.concatenate([oa_p, tail], axis=0), jnp.concatenate([og_p, tail], axis=0), hm,
                w_br_a[0].astype(BF16), w_br_b[0].astype(BF16), w_gates.astype(BF16), tm_row, 512)
    x1_m, h2_m = _outproj(mg, w_out[0].astype(BF16), xm, g_ffn[0], tm_row // 2, False, "outproj")

    xs = x_sample.reshape(n_sample, d)
    hs = _rms(xs, g_mix[0], F32, tm=n_sample)
    qk_s = _proj(hs, w, COL_QK, 2 * W_A, 512, n_sample, "qk", True, (g_qk,), name="proj_qk_s")
    z2_s = _proj(hs, w, COL_Z2, Z2_COLS, 512, n_sample, "plain", True, name="proj_z2_s")
    la_s = _proj(hs, w, COL_A, LANES, LANES, n_sample, "loga", True, (wfg_pad, bfg), name="proj_loga_s")
    sg_s = _proj(hs, w_gates, 0, 2 * d, 512, n_sample, "gate", True, name="proj_gate_s")

    q_s = (qk_s[:, :W_A] * (DH_A ** -0.5)).reshape(n_db, n_new, H_A, 2, DH_A)
    eye_h = jnp.eye(H_A, dtype=F32)
    eye_m = jnp.eye(2, dtype=F32)
    qbd = jnp.einsum("bthmd,hg,mn->bhmdgnt", q_s, eye_h, eye_m).reshape(n_db, W_A, LANES)
    col = jnp.arange(LANES)
    slc = slopes[col // (2 * n_new)].reshape(1, LANES)
    tcol = (col % n_new).astype(F32).reshape(1, LANES)
    oa_t = _decode_attention(page_table, sc, qbd, qk_s, z2_s, cache_k[0], cache_v[0], slc, tcol, g_sub[0])
    oa_s = jnp.swapaxes(oa_t, 1, 2).reshape(n_sample, W_A)
    og_s, s_sample = _gla(z2_s, la_s, state_gla[0], g_gla[0], n_db, n_new, 0, n_new, n_new, True, F32,
                          "gla_sample")
    mg_s = _merge_sample(oa_s, og_s, w_br_a[0], w_br_b[0], sg_s, 512)
    x1_s, h2_s = _outproj(mg_s, w_out[0], xs, g_ffn[0], n_sample, True, "outproj_sample")

    c0 = jnp.zeros((1, N_EXPERTS), F32)
    idx_m, gate_m, rank_m, cnt_m = _route(h2_m, n_prompt, w_router[0], b_router[0], c0, 512, False,
                                          "route_prompt")
    idx_s, gate_s, rank_s, cnt = _route(h2_s, n_sample, w_router[0], b_router[0], cnt_m, n_sample, True,
                                        "route_sample")
    eidx = jnp.concatenate([idx_m, idx_s], axis=0)
    gate = jnp.concatenate([gate_m, gate_s], axis=0)
    rank = jnp.concatenate([rank_m, rank_s], axis=0)
    h2 = jnp.concatenate([h2_m[:n_prompt], h2_s.astype(BF16)], axis=0)
    x1 = jnp.concatenate([x1_m[:n_prompt], x1_s], axis=0)

    tme = 256
    n_blocks = -(-(n_tok * TOP_K) // tme) + N_EXPERTS
    counts = cnt.reshape(N_EXPERTS).astype(jnp.int32)
    padded = (counts + tme - 1) // tme * tme
    pend = jnp.cumsum(padded)
    pstart = pend - padded
    dest = pstart[eidx] + rank
    n_used = pend[-1] // tme
    blk = jnp.arange(n_blocks, dtype=jnp.int32)
    be = jnp.minimum(jnp.searchsorted(pend, blk * tme, side="right"), N_EXPERTS - 1).astype(jnp.int32)
    be = jnp.where(blk < n_used, be, be[jnp.maximum(n_used - 1, 0)])
    first = jnp.concatenate([jnp.ones((1,), jnp.int32), (be[1:] != be[:-1]).astype(jnp.int32)])
    nu = n_used.reshape(1).astype(jnp.int32)
    tok = jnp.repeat(jnp.arange(n_tok, dtype=jnp.int32), TOP_K)
    tok_of_row = jnp.zeros((n_blocks * tme,), jnp.int32).at[dest.reshape(-1)].set(tok)

    xbuf = jnp.take(h2, tok_of_row, axis=0)
    act = _expert_up(be, first, nu, xbuf, w_e_up[0], b_e_up[0].reshape(N_EXPERTS, 1, -1), tme, 512)
    ybuf = _expert_down(be, first, nu, act, w_e_down[0], b_e_down[0].reshape(N_EXPERTS, 1, -1), tme, 1024)
    y = jnp.sum(jnp.take(ybuf, dest.reshape(-1), axis=0).reshape(n_tok, TOP_K, d) * gate[:, :, None], axis=1)
    out = x1 + y

    y_prompt = out[:n_prompt].reshape(n_batch, seq, d)
    y_sample = out[n_prompt:].reshape(n_db, n_new, d)
    k_meta = jnp.broadcast_to(qk[row_m:row_m + N_META, W_A:][None], (n_batch, N_META, W_A))
    v_meta = jnp.broadcast_to(z2[row_m:row_m + N_META, :W_A][None], (n_batch, N_META, W_A))
    k_prompt = jnp.concatenate([k_meta, qk[:n_prompt, W_A:].reshape(n_batch, seq, W_A)], axis=1)
    v_prompt = jnp.concatenate([v_meta, z2[:n_prompt, :W_A].reshape(n_batch, seq, W_A)], axis=1)
    k_prompt = k_prompt.reshape(1, n_batch, seq + N_META, H_A, VH_A)
    v_prompt = v_prompt.reshape(1, n_batch, seq + N_META, H_A, VH_A)
    k_sample = qk_s[:, W_A:].reshape(1, n_db, n_new, H_A, VH_A)
    v_sample = z2_s[:, :W_A].reshape(1, n_db, n_new, H_A, VH_A)
    return (y_prompt, y_sample, k_prompt, v_prompt, s_prompt[None], k_sample, v_sample, s_sample[None])
```

```python
import functools
import math

import jax
import jax.numpy as jnp
from jax import lax
from jax.experimental import pallas as pl
from jax.experimental.pallas import tpu as pltpu

F32 = jnp.float32
BF16 = jnp.bfloat16
HI = lax.Precision.HIGHEST

N_META = 16
H_A = 8
DH_A = 64
VH_A = 2 * DH_A
W_A = H_A * VH_A
H_B = 4
DK_B = 128
DV_B = 256
W_KB = H_B * DK_B
W_B = H_B * DV_B
GLA_RANK = 16
GLA_TAU = 16.0
GLA_CHUNK = 64
GLA_SUB = 16
N_EXPERTS = 32
TOP_K = 4
SWIGLU_LIMIT = 7.0
SWIGLU_ALPHA = 1.702
PAGE_SIZE = 128
EPS = 1e-6
LAM_INIT = 0.8 - 0.6 * math.exp(-0.3 * 0)

COL_QK = 0
COL_Z2 = 2 * W_A
Z2_COLS = W_A + 2 * W_KB + 2 * W_B
COL_A = COL_Z2 + Z2_COLS
COL_G = COL_A + GLA_RANK

LANES = 128
VMEM_LIMIT = 56 * 1024 * 1024

NEG = -1e30


def _mm(a, b, hi=False):
    if hi:
        return jnp.dot(a, b, precision=HI, preferred_element_type=F32)
    return jnp.dot(a.astype(BF16), b.astype(BF16), preferred_element_type=F32)


def _mm_nt(a, b, hi=False):
    dn = (((1,), (1,)), ((), ()))
    if hi:
        return lax.dot_general(a, b, dn, precision=HI, preferred_element_type=F32)
    return lax.dot_general(a.astype(BF16), b.astype(BF16), dn, preferred_element_type=F32)


def _mm_tn(a, b, hi=False):
    dn = (((0,), (0,)), ((), ()))
    if hi:
        return lax.dot_general(a, b, dn, precision=HI, preferred_element_type=F32)
    return lax.dot_general(a.astype(BF16), b.astype(BF16), dn, preferred_element_type=F32)


def _params(sem, vmem=VMEM_LIMIT):
    return pltpu.CompilerParams(dimension_semantics=sem, vmem_limit_bytes=vmem)


def _rms_kernel(x_ref, g_ref, o_ref):
    x = x_ref[...]
    ms = jnp.mean(x * x, axis=-1, keepdims=True)
    o_ref[...] = (x * lax.rsqrt(ms + EPS) * g_ref[...]).astype(o_ref.dtype)


def _rms(x, g, out_dtype, tm):
    m, d = x.shape
    return pl.pallas_call(
        _rms_kernel,
        grid=(m // tm,),
        in_specs=[pl.BlockSpec((tm, d), lambda i: (i, 0)),
                  pl.BlockSpec((1, d), lambda i: (0, 0))],
        out_specs=pl.BlockSpec((tm, d), lambda i: (i, 0)),
        out_shape=jax.ShapeDtypeStruct((m, d), out_dtype),
        compiler_params=_params(("arbitrary",)),
        name="rms_rows",
    )(x, g.reshape(1, d))


def _proj_kernel(*refs, mode, hi, tn):
    h_ref, w_ref = refs[0], refs[1]
    if mode == "qk":
        g_ref, o_ref = refs[2], refs[3]
        scr = refs[4:]
    elif mode == "loga":
        wfg_ref, bfg_ref, o_ref = refs[2], refs[3], refs[4]
        scr = refs[5:]
    else:
        o_ref = refs[2]
        scr = refs[3:]

    if hi:
        z = jnp.dot(h_ref[...], w_ref[...], precision=HI, preferred_element_type=F32)
    else:
        wbf = scr[0]

        @pl.when(pl.program_id(1) == 0)
        def _cast():
            wbf[...] = w_ref[...].astype(BF16)

        z = jnp.dot(h_ref[...], wbf[...], preferred_element_type=F32)

    if mode == "plain":
        o_ref[...] = z
    elif mode == "gate":
        o_ref[...] = jax.nn.sigmoid(z)
    elif mode == "qk":
        lane = lax.broadcasted_iota(jnp.int32, (z.shape[0], LANES), 1)
        lo = lane < DH_A
        g = g_ref[0]
        for c in range(tn // LANES):
            x = z[:, c * LANES:(c + 1) * LANES]
            x2 = x * x
            s0 = jnp.sum(jnp.where(lo, x2, 0.0), axis=-1, keepdims=True)
            s1 = jnp.sum(jnp.where(lo, 0.0, x2), axis=-1, keepdims=True)
            inv = jnp.where(lo, lax.rsqrt(s0 * (1.0 / DH_A) + EPS),
                            lax.rsqrt(s1 * (1.0 / DH_A) + EPS))
            o_ref[:, c * LANES:(c + 1) * LANES] = x * inv * g
    else:
        lane = lax.broadcasted_iota(jnp.int32, z.shape, 1)
        a = jnp.where(lane < GLA_RANK, z, 0.0)
        u = _mm(a, wfg_ref[...], hi) + bfg_ref[...]
        o_ref[...] = -(jnp.maximum(-u, 0.0) + jnp.log1p(jnp.exp(-jnp.abs(u)))) * (1.0 / GLA_TAU)


def _proj(h, w, col0, n_cols, tn, tm, mode, hi, extras=(), name="proj"):
    m, k = h.shape
    assert col0 % tn == 0 and n_cols % tn == 0 and m % tm == 0
    cb0 = col0 // tn
    in_specs = [pl.BlockSpec((tm, k), lambda j, i: (i, 0)),
                pl.BlockSpec((k, tn), lambda j, i: (0, cb0 + j))]
    out_cols, out_tn = n_cols, tn
    if mode == "qk":
        in_specs.append(pl.BlockSpec((1, 1, LANES), lambda j, i: ((j * tn) // W_A, 0, 0)))
    elif mode == "loga":
        in_specs += [pl.BlockSpec((LANES, W_KB), lambda j, i: (0, 0)),
                     pl.BlockSpec((1, W_KB), lambda j, i: (0, 0))]
        out_cols, out_tn = W_KB, W_KB
    return pl.pallas_call(
        functools.partial(_proj_kernel, mode=mode, hi=hi, tn=tn),
        grid=(n_cols // tn, m // tm),
        in_specs=in_specs,
        out_specs=pl.BlockSpec((tm, out_tn), lambda j, i: (i, j)),
        out_shape=jax.ShapeDtypeStruct((m, out_cols), F32),
        scratch_shapes=[] if hi else [pltpu.VMEM((k, tn), BF16)],
        compiler_params=_params(("arbitrary", "arbitrary")),
        name=name,
    )(h, w, *extras)


def _attn_kernel(sc_ref, q_ref, k_ref, v_ref, km_ref, vm_ref, gs_ref, o_ref, kbf, vbf, m_s, l_s, acc_s, *,
                 tq, tk, n_heads):
    hg = pl.program_id(1)
    i = pl.program_id(2)
    half = tk // 2
    assert half == tq

    @pl.when(i == 0)
    def _cast():
        kbf[...] = k_ref[...].astype(BF16)
        vbf[...] = v_ref[...].astype(BF16)

    lam = sc_ref[0]
    q0 = i * tq
    width = 2 * tq
    lane = lax.broadcasted_iota(jnp.int32, (tq, LANES), 1)
    lo = lane < DH_A

    def tile_rel(n_keys):
        key = lax.broadcasted_iota(jnp.int32, (n_keys, width), 0)
        qry = lax.broadcasted_iota(jnp.int32, (n_keys, width), 1) % tq
        return key, qry, (qry - key).astype(F32)

    _, _, rel_full = tile_rel(tk)
    key, qry, rel_half = tile_rel(tq)

    def head_cols(g):
        return slice(g * LANES, (g + 1) * LANES)

    slopes, q12s = [], []
    for g in range(n_heads):
        slopes.append(sc_ref[1 + hg * n_heads + g])
        q = q_ref[:, head_cols(g)] * (DH_A ** -0.5)
        q12s.append(jnp.concatenate([jnp.where(lo, q, 0.0), jnp.where(lo, 0.0, q)], axis=0).astype(BF16))

    def scores(g, kc, n_keys, k0):
        s = lax.dot_general(kc, q12s[g], (((1,), (1,)), ((), ())), preferred_element_type=F32)
        rel = rel_full if n_keys == tk else rel_half
        return s + (rel + (q0 - k0).astype(F32)) * (-slopes[g])

    def update(g, s, vc):
        m = m_s[g]
        m_new = jnp.maximum(m, jnp.max(s, axis=0, keepdims=True))
        alpha = jnp.exp(m - m_new)
        p = jnp.exp(s - m_new)
        l_s[g] = alpha * l_s[g] + jnp.sum(p, axis=0, keepdims=True)
        acc_s[g] = alpha * acc_s[g] + _mm_tn(vc, p)
        m_s[g] = m_new

    for g in range(n_heads):
        s = scores(g, km_ref[:, head_cols(g)].astype(BF16), tq, -N_META)
        s = jnp.where(key < N_META, s, NEG)
        m0 = jnp.max(s, axis=0, keepdims=True)
        p = jnp.exp(s - m0)
        m_s[g] = m0
        l_s[g] = jnp.sum(p, axis=0, keepdims=True)
        acc_s[g] = _mm_tn(vm_ref[:, head_cols(g)], p)

    def full_chunk(j, carry):
        k0 = pl.multiple_of(j * tk, tk)
        for g in range(n_heads):
            update(g, scores(g, kbf[pl.ds(k0, tk), head_cols(g)], tk, k0), vbf[pl.ds(k0, tk), head_cols(g)])
        return carry

    def half_chunk(j, carry):
        k0 = pl.multiple_of(q0 - half, half)
        for g in range(n_heads):
            update(g, scores(g, kbf[pl.ds(k0, half), head_cols(g)], half, k0),
                   vbf[pl.ds(k0, half), head_cols(g)])
        return carry

    lax.fori_loop(0, q0 // tk, full_chunk, 0)
    lax.fori_loop(0, i % 2, half_chunk, 0)
    kq = pl.multiple_of(q0, tq)
    for g in range(n_heads):
        s = scores(g, kbf[pl.ds(kq, tq), head_cols(g)], tq, kq)
        s = jnp.where(qry >= key, s, NEG)
        update(g, s, vbf[pl.ds(kq, tq), head_cols(g)])
        accn = acc_s[g] / l_s[g]
        o = accn[:, :tq] - lam * accn[:, tq:]
        ms = jnp.mean(o * o, axis=0, keepdims=True)
        y = o * lax.rsqrt(ms + EPS) * gs_ref[...] * (1.0 - LAM_INIT)
        o_ref[:, head_cols(g)] = y.T.astype(o_ref.dtype)


def _prompt_attention(sc, qk, z2, g_sub, n_batch, seq, meta_row0, tq=128, tk=256, n_heads=8):
    nq = seq // tq
    n_prompt = n_batch * seq
    assert meta_row0 % tq == 0 and qk.shape[0] >= meta_row0 + tq and H_A % n_heads == 0
    meta_blk = meta_row0 // tq
    gw = n_heads * LANES
    kcol0 = W_A // gw
    return pl.pallas_call(
        functools.partial(_attn_kernel, tq=tq, tk=tk, n_heads=n_heads),
        grid=(n_batch, H_A // n_heads, nq),
        in_specs=[
            pl.BlockSpec(memory_space=pltpu.SMEM),
            pl.BlockSpec((tq, gw), lambda b, h, i: (b * nq + i, h)),
            pl.BlockSpec((seq, gw), lambda b, h, i: (b, kcol0 + h)),
            pl.BlockSpec((seq, gw), lambda b, h, i: (b, h)),
            pl.BlockSpec((tq, gw), lambda b, h, i: (meta_blk, kcol0 + h)),
            pl.BlockSpec((tq, gw), lambda b, h, i: (meta_blk, h)),
            pl.BlockSpec((VH_A, 1), lambda b, h, i: (0, 0)),
        ],
        out_specs=pl.BlockSpec((tq, gw), lambda b, h, i: (b * nq + i, h)),
        out_shape=jax.ShapeDtypeStruct((n_prompt, W_A), BF16),
        scratch_shapes=[pltpu.VMEM((seq, gw), BF16), pltpu.VMEM((seq, gw), BF16),
                        pltpu.VMEM((n_heads, 1, 2 * tq), F32), pltpu.VMEM((n_heads, 1, 2 * tq), F32),
                        pltpu.VMEM((n_heads, VH_A, 2 * tq), F32)],
        compiler_params=_params(("arbitrary", "arbitrary", "arbitrary")),
        name="prompt_attention",
    )(sc, qk, qk, z2, qk, z2, g_sub.reshape(VH_A, 1))


def _dec_kernel(pt_ref, sc_ref, qbd_ref, kn_ref, vn_ref, slc_ref, tcol_ref, gs_ref, *rest,
                g_pages, n_steps, past, n_new):
    kp_refs = rest[:g_pages]
    vp_refs = rest[g_pages:2 * g_pages]
    o_ref = rest[2 * g_pages]
    m_s, l_s, acc_s = rest[2 * g_pages + 1:]
    j = pl.program_id(1)

    @pl.when(j == 0)
    def _init():
        m_s[...] = jnp.full(m_s.shape, NEG, F32)
        l_s[...] = jnp.zeros(l_s.shape, F32)
        acc_s[...] = jnp.zeros(acc_s.shape, F32)

    qcat = qbd_ref[...]
    slc = slc_ref[...]
    tcol = tcol_ref[...]
    keyrow = lax.broadcasted_iota(jnp.int32, (PAGE_SIZE, LANES), 0).astype(F32)
    rel = tcol - keyrow

    def split(x):
        hi = x.astype(BF16)
        return hi, (x - hi.astype(F32)).astype(BF16)

    def update(s_all, pv_fn):
        m_old = m_s[...]
        m_new = jnp.maximum(m_old, jnp.max(s_all, axis=0, keepdims=True))
        alpha = jnp.exp(m_old - m_new)
        p = jnp.exp(s_all - m_new)
        l_s[...] = alpha * l_s[...] + jnp.sum(p, axis=0, keepdims=True)
        acc_s[...] = acc_s[...] * alpha + pv_fn(p)
        m_s[...] = m_new

    ss = []
    for u in range(g_pages):
        page0 = (past - (j * g_pages + u) * PAGE_SIZE).astype(F32)
        k_hi, k_lo = split(kp_refs[u][...])
        s2 = jnp.dot(k_hi, qcat, preferred_element_type=F32)
        s = s2[:, :LANES] + s2[:, LANES:] + jnp.dot(k_lo, qcat[:, :LANES], preferred_element_type=F32)
        ss.append(s + (rel + page0) * (-slc))
    s_all = jnp.concatenate(ss, axis=0)

    def pv_pages(p):
        p_hi, p_lo = split(p)
        pcat = jnp.concatenate([p_hi, p_lo], axis=1)
        out = None
        for u in range(g_pages):
            rows = slice(u * PAGE_SIZE, (u + 1) * PAGE_SIZE)
            v_hi, v_lo = split(vp_refs[u][...])
            t2 = _mm_tn(v_hi, pcat[rows])
            t = t2[:, :LANES] + t2[:, LANES:] + _mm_tn(v_lo, p_hi[rows])
            out = t if out is None else out + t
        return out

    update(s_all, pv_pages)

    @pl.when(j == n_steps - 1)
    def _finish():
        qbd = qcat[:, :LANES].astype(F32) + qcat[:, LANES:].astype(F32)
        trow = lax.broadcasted_iota(jnp.int32, (n_new, LANES), 0).astype(F32)
        s = jnp.dot(kn_ref[...], qbd, precision=HI, preferred_element_type=F32)
        s = s + (tcol - trow) * (-slc)
        s = jnp.where(trow <= tcol, s, NEG)
        update(s, lambda p: _mm_tn(vn_ref[...], p, True))

        lam = sc_ref[0]
        accn = acc_s[...] / l_s[...]
        for h in range(H_A):
            blk = accn[h * VH_A:(h + 1) * VH_A, h * 2 * n_new:(h + 1) * 2 * n_new]
            d = blk[:, :n_new] - lam * blk[:, n_new:]
            ms = jnp.mean(d * d, axis=0, keepdims=True)
            o_ref[h * VH_A:(h + 1) * VH_A, :] = d * lax.rsqrt(ms + EPS) * gs_ref[...] * (1.0 - LAM_INIT)


def _decode_attention(page_table, sc, qbd, qk_s, z2_s, cache_k, cache_v, slc, tcol, g_sub, g_pages=8):
    n_db, n_pages = page_table.shape
    n_new = qk_s.shape[0] // n_db
    past = n_pages * PAGE_SIZE
    n_steps = n_pages // g_pages
    n_pool = cache_k.shape[-4]
    width = H_A * VH_A
    ck = cache_k.reshape(n_pool, PAGE_SIZE, width)
    cv = cache_v.reshape(n_pool, PAGE_SIZE, width)
    kcol0 = W_A // width
    q_hi = qbd.astype(BF16)
    qcat = jnp.concatenate([q_hi, (qbd - q_hi.astype(F32)).astype(BF16)], axis=-1)

    def page_spec(u):
        return pl.BlockSpec((None, PAGE_SIZE, width),
                            lambda d, j, pt: (pt[d * n_pages + j * g_pages + u], 0, 0))

    in_specs = [
        pl.BlockSpec(memory_space=pltpu.SMEM),
        pl.BlockSpec((None, width, 2 * LANES), lambda d, j, pt: (d, 0, 0)),
        pl.BlockSpec((n_new, width), lambda d, j, pt: (d, kcol0)),
        pl.BlockSpec((n_new, width), lambda d, j, pt: (d, 0)),
        pl.BlockSpec((1, LANES), lambda d, j, pt: (0, 0)),
        pl.BlockSpec((1, LANES), lambda d, j, pt: (0, 0)),
        pl.BlockSpec((VH_A, 1), lambda d, j, pt: (0, 0)),
    ] + [page_spec(u) for u in range(g_pages)] * 2
    grid_spec = pltpu.PrefetchScalarGridSpec(
        num_scalar_prefetch=1,
        grid=(n_db, n_steps),
        in_specs=in_specs,
        out_specs=pl.BlockSpec((None, width, n_new), lambda d, j, pt: (d, 0, 0)),
        scratch_shapes=[pltpu.VMEM((1, LANES), F32), pltpu.VMEM((1, LANES), F32),
                        pltpu.VMEM((width, LANES), F32)],
    )
    return pl.pallas_call(
        functools.partial(_dec_kernel, g_pages=g_pages, n_steps=n_steps, past=past, n_new=n_new),
        grid_spec=grid_spec,
        out_shape=jax.ShapeDtypeStruct((n_db, width, n_new), F32),
        compiler_params=_params(("arbitrary", "arbitrary")),
        name="decode_attention",
    )(page_table.reshape(-1), sc, qcat, qk_s, z2_s, slc, tcol, g_sub.reshape(VH_A, 1),
      *([ck] * g_pages), *([cv] * g_pages))


def _gla_kernel(q_ref, k_ref, v_ref, la_ref, r_ref, s0_ref, gg_ref, o_ref, so_ref, state, *,
                chunk, sub, n_chunks, hi):
    state[...] = s0_ref[...]
    row = lax.broadcasted_iota(jnp.int32, (chunk, chunk), 0)
    col = lax.broadcasted_iota(jnp.int32, (chunk, chunk), 1)
    tri = (col <= row).astype(F32)
    tri_sub = (col <= (row // sub) * sub + (sub - 1)).astype(F32)
    causal = col <= row
    colblk = col // sub
    rowc = lax.broadcasted_iota(jnp.int32, (chunk, DK_B), 0)
    ones = jnp.ones((chunk, DK_B), F32)
    n_sub = chunk // sub

    def body(c, carry):
        c0 = pl.multiple_of(c * chunk, chunk)
        sl = pl.ds(c0, chunk)
        la = la_ref[sl, :]
        b = jnp.dot(tri, la, precision=HI, preferred_element_type=F32)
        rsub = jnp.dot(tri_sub, la, precision=HI, preferred_element_type=F32)
        b_end = b[chunk - 1:chunk, :]
        qs = q_ref[sl, :] * (DK_B ** -0.5)
        kk = k_ref[sl, :]
        vv = v_ref[sl, :]
        s_old = state[...]

        inter = _mm(qs * jnp.exp(b), s_old, hi)

        kt = kk * jnp.exp(rsub - b)
        qj = []
        for jb in range(n_sub):
            ref = rsub[jb * sub:jb * sub + 1, :]
            qj.append(qs * jnp.exp(jnp.where(rowc >= jb * sub, b - ref, -jnp.inf)))
        a4 = _mm_nt(jnp.concatenate(qj, axis=0), kt, hi)
        att = a4[:chunk]
        for jb in range(1, n_sub):
            att = jnp.where(colblk == jb, a4[jb * chunk:(jb + 1) * chunk], att)
        att = jnp.where(causal, att, 0.0)
        o = inter + _mm(att, vv, hi)

        dec = jnp.exp(_mm_tn(la, ones, True))
        kd = kk * jnp.exp(b_end - b)
        state[...] = jnp.concatenate([dec] * (DV_B // DK_B), axis=1) * s_old + _mm_tn(kd, vv, hi)

        ms = jnp.mean(o * o, axis=-1, keepdims=True)
        r = r_ref[sl, :]
        y = o * lax.rsqrt(ms + EPS) * gg_ref[...] * (r * jax.nn.sigmoid(r))
        o_ref[sl, :] = y.astype(o_ref.dtype)
        return carry

    lax.fori_loop(0, n_chunks, body, 0)
    so_ref[...] = state[...]


def _gla(z2, la, s0, g_gla, n_batch, seq, row0, chunk, sub, hi, out_dtype, name):
    rb0 = row0 // seq
    qc0 = W_A // DK_B
    kc0 = (W_A + W_KB) // DK_B
    vc0 = (W_A + 2 * W_KB) // DV_B
    rc0 = (W_A + 2 * W_KB + W_B) // DV_B
    per_batch_state = s0.shape[0] != 1
    return pl.pallas_call(
        functools.partial(_gla_kernel, chunk=chunk, sub=sub, n_chunks=seq // chunk, hi=hi),
        grid=(n_batch, H_B),
        in_specs=[
            pl.BlockSpec((seq, DK_B), lambda b, h: (rb0 + b, qc0 + h)),
            pl.BlockSpec((seq, DK_B), lambda b, h: (rb0 + b, kc0 + h)),
            pl.BlockSpec((seq, DV_B), lambda b, h: (rb0 + b, vc0 + h)),
            pl.BlockSpec((seq, DK_B), lambda b, h: (rb0 + b, h)),
            pl.BlockSpec((seq, DV_B), lambda b, h: (rb0 + b, rc0 + h)),
            pl.BlockSpec((None, None, DK_B, DV_B),
                         (lambda b, h: (b, h, 0, 0)) if per_batch_state else (lambda b, h: (0, h, 0, 0))),
            pl.BlockSpec((1, DV_B), lambda b, h: (0, 0)),
        ],
        out_specs=[
            pl.BlockSpec((seq, DV_B), lambda b, h: (b, h)),
            pl.BlockSpec((None, None, DK_B, DV_B), lambda b, h: (b, h, 0, 0)),
        ],
        out_shape=[jax.ShapeDtypeStruct((n_batch * seq, W_B), out_dtype),
                   jax.ShapeDtypeStruct((n_batch, H_B, DK_B, DV_B), F32)],
        scratch_shapes=[pltpu.VMEM((DK_B, DV_B), F32)],
        compiler_params=_params(("arbitrary", "arbitrary")),
        name=name,
    )(z2, z2, z2, la, z2, s0, g_gla.reshape(1, DV_B))


def _merge_kernel(oa_ref, og_ref, h_ref, wa_ref, wb_ref, wga_ref, wgb_ref, o_ref):
    h = h_ref[...]
    pa = jnp.dot(oa_ref[...], wa_ref[...], preferred_element_type=F32)
    ga = jax.nn.sigmoid(jnp.dot(h, wga_ref[...], preferred_element_type=F32))
    acc = ga * pa
    pb = jnp.dot(og_ref[...], wb_ref[...], preferred_element_type=F32)
    gb = jax.nn.sigmoid(jnp.dot(h, wgb_ref[...], preferred_element_type=F32))
    o_ref[...] = (acc + gb * pb).astype(o_ref.dtype)


def _merge(oa, og, h, wa, wb, wg, tm, tn):
    m, k = h.shape
    d = wa.shape[1]
    gb0 = d // tn
    return pl.pallas_call(
        _merge_kernel,
        grid=(d // tn, m // tm),
        in_specs=[
            pl.BlockSpec((tm, W_A), lambda j, i: (i, 0)),
            pl.BlockSpec((tm, W_B), lambda j, i: (i, 0)),
            pl.BlockSpec((tm, k), lambda j, i: (i, 0)),
            pl.BlockSpec((W_A, tn), lambda j, i: (0, j)),
            pl.BlockSpec((W_B, tn), lambda j, i: (0, j)),
            pl.BlockSpec((k, tn), lambda j, i: (0, j)),
            pl.BlockSpec((k, tn), lambda j, i: (0, gb0 + j)),
        ],
        out_specs=pl.BlockSpec((tm, tn), lambda j, i: (i, j)),
        out_shape=jax.ShapeDtypeStruct((m, d), BF16),
        compiler_params=_params(("arbitrary", "arbitrary")),
        name="merge",
    )(oa, og, h, wa, wb, wg, wg)


def _merge_s_kernel(oa_ref, og_ref, wa_ref, wb_ref, ga_ref, gb_ref, o_ref):
    pa = _mm(oa_ref[...], wa_ref[...], True)
    pb = _mm(og_ref[...], wb_ref[...], True)
    o_ref[...] = ga_ref[...] * pa + gb_ref[...] * pb


def _merge_sample(oa, og, wa, wb, sg, tn):
    m = oa.shape[0]
    d = wa.shape[1]
    gb0 = d // tn
    return pl.pallas_call(
        _merge_s_kernel,
        grid=(d // tn,),
        in_specs=[
            pl.BlockSpec((m, W_A), lambda j: (0, 0)),
            pl.BlockSpec((m, W_B), lambda j: (0, 0)),
            pl.BlockSpec((W_A, tn), lambda j: (0, j)),
            pl.BlockSpec((W_B, tn), lambda j: (0, j)),
            pl.BlockSpec((m, tn), lambda j: (0, j)),
            pl.BlockSpec((m, tn), lambda j: (0, gb0 + j)),
        ],
        out_specs=pl.BlockSpec((m, tn), lambda j: (0, j)),
        out_shape=jax.ShapeDtypeStruct((m, d), F32),
        compiler_params=_params(("arbitrary",)),
        name="merge_sample",
    )(oa, og, wa, wb, sg, sg)


def _outproj_kernel(mg_ref, w_ref, x_ref, gf_ref, x1_ref, h2_ref, *, hi):
    x1 = x_ref[...] + _mm(mg_ref[...], w_ref[...], hi)
    x1_ref[...] = x1
    ms = jnp.mean(x1 * x1, axis=-1, keepdims=True)
    h2_ref[...] = (x1 * lax.rsqrt(ms + EPS) * gf_ref[...]).astype(h2_ref.dtype)


def _outproj(mg, w, x, g_ffn, tm, hi, name):
    m, d = x.shape
    return pl.pallas_call(
        functools.partial(_outproj_kernel, hi=hi),
        grid=(m // tm,),
        in_specs=[
            pl.BlockSpec((tm, d), lambda i: (i, 0)),
            pl.BlockSpec((d, d), lambda i: (0, 0)),
            pl.BlockSpec((tm, d), lambda i: (i, 0)),
            pl.BlockSpec((1, d), lambda i: (0, 0)),
        ],
        out_specs=[pl.BlockSpec((tm, d), lambda i: (i, 0)), pl.BlockSpec((tm, d), lambda i: (i, 0))],
        out_shape=[jax.ShapeDtypeStruct((m, d), F32), jax.ShapeDtypeStruct((m, d), F32 if hi else BF16)],
        compiler_params=_params(("arbitrary",)),
        name=name,
    )(mg, w, x, g_ffn.reshape(1, d))


def _route_kernel(h_ref, w_ref, b_ref, c0_ref, idx_ref, gate_ref, rank_ref, cnt_ref, carry, *, tm, hi):
    @pl.when(pl.program_id(0) == 0)
    def _init():
        carry[...] = c0_ref[...]

    logits = _mm(h_ref[...], w_ref[...], hi) + b_ref[...]
    lane = lax.broadcasted_iota(jnp.int32, (tm, N_EXPERTS), 1).astype(F32)
    work = logits
    vals, idxs, sels = [], [], []
    for _ in range(TOP_K):
        mx = jnp.max(work, axis=-1, keepdims=True)
        ix = jnp.min(jnp.where(work == mx, lane, float(N_EXPERTS)), axis=-1, keepdims=True)
        sel = lane == ix
        vals.append(mx)
        idxs.append(ix)
        sels.append(sel)
        work = jnp.where(sel, -jnp.inf, work)
    es = [jnp.exp(v - vals[0]) for v in vals]
    tot = es[0] + es[1] + es[2] + es[3]
    gate_ref[...] = jnp.concatenate([e / tot for e in es], axis=1)
    idx_ref[...] = jnp.concatenate(idxs, axis=1).astype(jnp.int32)

    onehot = jnp.zeros((tm, N_EXPERTS), F32)
    for sel in sels:
        onehot = onehot + sel.astype(F32)
    r = lax.broadcasted_iota(jnp.int32, (tm, tm), 0)
    c = lax.broadcasted_iota(jnp.int32, (tm, tm), 1)
    before = (c < r).astype(BF16)
    cum = jnp.dot(before, onehot.astype(BF16), preferred_element_type=F32) + carry[...]
    ranks = [jnp.sum(jnp.where(sel, cum, 0.0), axis=-1, keepdims=True) for sel in sels]
    rank_ref[...] = jnp.concatenate(ranks, axis=1).astype(jnp.int32)
    carry[...] = carry[...] + jnp.sum(onehot, axis=0, keepdims=True)
    cnt_ref[...] = carry[...]


def _route(h, n_tok, w_router, b_router, c0, tm, hi, name):
    d = h.shape[1]
    assert n_tok % tm == 0
    return pl.pallas_call(
        functools.partial(_route_kernel, tm=tm, hi=hi),
        grid=(n_tok // tm,),
        in_specs=[
            pl.BlockSpec((tm, d), lambda i: (i, 0)),
            pl.BlockSpec((d, N_EXPERTS), lambda i: (0, 0)),
            pl.BlockSpec((1, N_EXPERTS), lambda i: (0, 0)),
            pl.BlockSpec((1, N_EXPERTS), lambda i: (0, 0)),
        ],
        out_specs=[
            pl.BlockSpec((tm, TOP_K), lambda i: (i, 0)),
            pl.BlockSpec((tm, TOP_K), lambda i: (i, 0)),
            pl.BlockSpec((tm, TOP_K), lambda i: (i, 0)),
            pl.BlockSpec((1, N_EXPERTS), lambda i: (0, 0)),
        ],
        out_shape=[
            jax.ShapeDtypeStruct((n_tok, TOP_K), jnp.int32),
            jax.ShapeDtypeStruct((n_tok, TOP_K), F32),
            jax.ShapeDtypeStruct((n_tok, TOP_K), jnp.int32),
            jax.ShapeDtypeStruct((1, N_EXPERTS), F32),
        ],
        scratch_shapes=[pltpu.VMEM((1, N_EXPERTS), F32)],
        compiler_params=_params(("arbitrary",)),
        name=name,
    )(h, w_router, b_router.reshape(1, N_EXPERTS), c0)


def _up_kernel(be_ref, first_ref, nu_ref, x_ref, wg_ref, wl_ref, bg_ref, bl_ref, o_ref, wgb, wlb):
    b = pl.program_id(1)

    @pl.when(first_ref[b] == 1)
    def _cast():
        wgb[...] = wg_ref[...].astype(BF16)
        wlb[...] = wl_ref[...].astype(BF16)

    @pl.when(b < nu_ref[0])
    def _compute():
        x = x_ref[...]
        g = jnp.dot(x, wgb[...], preferred_element_type=F32) + bg_ref[...]
        lin = jnp.dot(x, wlb[...], preferred_element_type=F32) + bl_ref[...]
        g = jnp.minimum(g, SWIGLU_LIMIT)
        lin = jnp.clip(lin, -SWIGLU_LIMIT, SWIGLU_LIMIT)
        glu = g * jax.nn.sigmoid(SWIGLU_ALPHA * g)
        o_ref[...] = ((lin + 1.0) * glu).astype(o_ref.dtype)


def _expert_up(be, first, nu, xbuf, w_up, b_up, tm, th):
    rows, d = xbuf.shape
    d_e = w_up.shape[2] // 2
    nj = d_e // th
    grid_spec = pltpu.PrefetchScalarGridSpec(
        num_scalar_prefetch=3,
        grid=(nj, rows // tm),
        in_specs=[
            pl.BlockSpec((tm, d), lambda j, b, be, fi, nu: (b, 0)),
            pl.BlockSpec((None, d, th), lambda j, b, be, fi, nu: (be[b], 0, j)),
            pl.BlockSpec((None, d, th), lambda j, b, be, fi, nu: (be[b], 0, nj + j)),
            pl.BlockSpec((None, 1, th), lambda j, b, be, fi, nu: (be[b], 0, j)),
            pl.BlockSpec((None, 1, th), lambda j, b, be, fi, nu: (be[b], 0, nj + j)),
        ],
        out_specs=pl.BlockSpec((tm, th), lambda j, b, be, fi, nu: (b, j)),
        scratch_shapes=[pltpu.VMEM((d, th), BF16), pltpu.VMEM((d, th), BF16)],
    )
    return pl.pallas_call(
        _up_kernel,
        grid_spec=grid_spec,
        out_shape=jax.ShapeDtypeStruct((rows, d_e), BF16),
        compiler_params=_params(("arbitrary", "arbitrary")),
        name="expert_up",
    )(be, first, nu, xbuf, w_up, w_up, b_up, b_up)


def _down_kernel(be_ref, first_ref, nu_ref, a_ref, w_ref, b_ref, o_ref, wb):
    b = pl.program_id(1)

    @pl.when(first_ref[b] == 1)
    def _cast():
        wb[...] = w_ref[...].astype(BF16)

    @pl.when(b < nu_ref[0])
    def _compute():
        o_ref[...] = jnp.dot(a_ref[...], wb[...], preferred_element_type=F32) + b_ref[...]


def _expert_down(be, first, nu, act, w_down, b_down, tm, tn):
    rows, d_e = act.shape
    d = w_down.shape[2]
    grid_spec = pltpu.PrefetchScalarGridSpec(
        num_scalar_prefetch=3,
        grid=(d // tn, rows // tm),
        in_specs=[
            pl.BlockSpec((tm, d_e), lambda j, b, be, fi, nu: (b, 0)),
            pl.BlockSpec((None, d_e, tn), lambda j, b, be, fi, nu: (be[b], 0, j)),
            pl.BlockSpec((None, 1, tn), lambda j, b, be, fi, nu: (be[b], 0, j)),
        ],
        out_specs=pl.BlockSpec((tm, tn), lambda j, b, be, fi, nu: (b, j)),
        scratch_shapes=[pltpu.VMEM((d_e, tn), BF16)],
    )
    return pl.pallas_call(
        _down_kernel,
        grid_spec=grid_spec,
        out_shape=jax.ShapeDtypeStruct((rows, d), F32),
        compiler_params=_params(("arbitrary", "arbitrary")),
        name="expert_down",
    )(be, first, nu, act, w_down, b_down)


def kernel(x_prompt, x_sample, cache_k, cache_v, state_gla, page_table, meta_tokens, g_mix, w_in, g_q, g_k, lam_q1, lam_k1, lam_q2, lam_k2, g_sub, w_fg, b_fg, g_gla, w_br_a, w_br_b, w_out, g_ffn, w_router, b_router, w_e_up, b_e_up, w_e_down, b_e_down):
    assert w_in.shape[0] == 1, "single-layer trunk"
    n_batch, seq, d = x_prompt.shape
    n_db, n_new, _ = x_sample.shape
    assert 2 * H_A * n_new == LANES
    n_prompt = n_batch * seq
    n_sample = n_db * n_new
    n_tok = n_prompt + n_sample
    row_m = n_prompt

    def layer0(x):
        return x.reshape(x.shape[1:])

    w = layer0(w_in)
    w_a, w_b, w_o, w_r = layer0(w_br_a), layer0(w_br_b), layer0(w_out), layer0(w_router)
    lam =(jnp.exp(jnp.sum(lam_q1[0] * lam_k1[0])) - jnp.exp(jnp.sum(lam_q2[0] * lam_k2[0])) + LAM_INIT)
    slopes = jnp.exp2(-8.0 * jnp.arange(1, H_A + 1, dtype=F32) / H_A)
    sc = jnp.concatenate([lam.reshape(1), slopes]).astype(F32)
    g_qk = jnp.stack([g_q[0].reshape(1, VH_A), g_k[0].reshape(1, VH_A)])
    wfg_pad = jnp.zeros((LANES, W_KB), F32).at[:GLA_RANK].set(w_fg[0])
    bfg = b_fg[0].reshape(1, W_KB)
    w_gates = w[:, COL_G:]

    pad_rows = (-(n_prompt + N_META)) % (5 * 13 * LANES)
    xm = jnp.concatenate([x_prompt.reshape(n_prompt, d), meta_tokens.astype(F32),
                          jnp.zeros((pad_rows, d), F32)], axis=0)
    n_rows = xm.shape[0]
    tm_proj = n_rows // 5
    tm_row = n_rows // 13
    hm = _rms(xm, g_mix[0], BF16, tm=tm_row)
    qk = _proj(hm, w, COL_QK, 2 * W_A, 512, tm_proj, "qk", False, (g_qk,), name="proj_qk")
    z2 = _proj(hm, w, COL_Z2, Z2_COLS, 512, tm_proj, "plain", False, name="proj_z2")
    la = _proj(hm, w, COL_A, LANES, LANES, tm_proj, "loga", False, (wfg_pad, bfg), name="proj_loga")

    oa_p = _prompt_attention(sc, qk, z2, g_sub[0], n_batch, seq, row_m)
    zero_state = jnp.zeros((1, H_B, DK_B, DV_B), F32)
    _, s_meta = _gla(z2, la, zero_state, g_gla[0], 1, N_META, row_m, N_META, N_META, False, BF16, "gla_meta")
    og_p, s_prompt = _gla(z2, la, s_meta, g_gla[0], n_batch, seq, 0, GLA_CHUNK, GLA_SUB, False, BF16,
                          "gla_prompt")
    tail = jnp.zeros((n_rows - n_prompt, W_A), BF16)
    mg = _merge(jnp.concatenate([oa_p, tail], axis=0), jnp.concatenate([og_p, tail], axis=0), hm,
                w_a.astype(BF16), w_b.astype(BF16), w_gates.astype(BF16), tm_row, 512)
    x1_m, h2_m = _outproj(mg, w_o.astype(BF16), xm, g_ffn[0], tm_row // 2, False, "outproj")

    xs = x_sample.reshape(n_sample, d)
    hs = _rms(xs, g_mix[0], F32, tm=n_sample)
    qk_s = _proj(hs, w, COL_QK, 2 * W_A, 512, n_sample, "qk", True, (g_qk,), name="proj_qk_s")
    z2_s = _proj(hs, w, COL_Z2, Z2_COLS, 512, n_sample, "plain", True, name="proj_z2_s")
    la_s = _proj(hs, w, COL_A, LANES, LANES, n_sample, "loga", True, (wfg_pad, bfg), name="proj_loga_s")
    sg_s = _proj(hs, w_gates, 0, 2 * d, 512, n_sample, "gate", True, name="proj_gate_s")

    q_s = (qk_s[:, :W_A] * (DH_A ** -0.5)).reshape(n_db, n_new, H_A, 2, DH_A)
    eye_h = jnp.eye(H_A, dtype=F32)
    eye_m = jnp.eye(2, dtype=F32)
    qbd = jnp.einsum("bthmd,hg,mn->bhmdgnt", q_s, eye_h, eye_m).reshape(n_db, W_A, LANES)
    col = jnp.arange(LANES)
    slc = slopes[col // (2 * n_new)].reshape(1, LANES)
    tcol = (col % n_new).astype(F32).reshape(1, LANES)
    oa_t = _decode_attention(page_table, sc, qbd, qk_s, z2_s, cache_k, cache_v, slc, tcol, g_sub[0])
    oa_s = jnp.swapaxes(oa_t, 1, 2).reshape(n_sample, W_A)
    og_s, s_sample = _gla(z2_s, la_s, layer0(state_gla), g_gla[0], n_db, n_new, 0, n_new, n_new, True, F32,
                          "gla_sample")
    mg_s = _merge_sample(oa_s, og_s, w_a, w_b, sg_s, 512)
    x1_s, h2_s = _outproj(mg_s, w_o, xs, g_ffn[0], n_sample, True, "outproj_sample")

    c0 = jnp.zeros((1, N_EXPERTS), F32)
    idx_m, gate_m, rank_m, cnt_m = _route(h2_m, n_prompt, w_r, b_router[0], c0, 512, False, "route_prompt")
    idx_s, gate_s, rank_s, cnt = _route(h2_s, n_sample, w_r, b_router[0], cnt_m, n_sample, True,
                                        "route_sample")
    eidx = jnp.concatenate([idx_m, idx_s], axis=0)
    gate = jnp.concatenate([gate_m, gate_s], axis=0)
    rank = jnp.concatenate([rank_m, rank_s], axis=0)
    h2 = jnp.concatenate([h2_m[:n_prompt], h2_s.astype(BF16)], axis=0)
    x1 = jnp.concatenate([x1_m[:n_prompt], x1_s], axis=0)

    tme = 256
    n_blocks = -(-(n_tok * TOP_K) // tme) + N_EXPERTS
    counts = cnt.reshape(N_EXPERTS).astype(jnp.int32)
    padded = (counts + tme - 1) // tme * tme
    pend = jnp.cumsum(padded)
    pstart = pend - padded
    dest = pstart[eidx] + rank
    n_used = pend[-1] // tme
    blk = jnp.arange(n_blocks, dtype=jnp.int32)
    be = jnp.minimum(jnp.sum(pend[None, :] <= (blk * tme)[:, None], axis=1), N_EXPERTS - 1).astype(jnp.int32)
    be = jnp.where(blk < n_used, be, be[jnp.maximum(n_used - 1, 0)])
    first = jnp.concatenate([jnp.ones((1,), jnp.int32), (be[1:] != be[:-1]).astype(jnp.int32)])
    nu = n_used.reshape(1).astype(jnp.int32)
    tok = jnp.repeat(jnp.arange(n_tok, dtype=jnp.int32), TOP_K)
    tok_of_row = jnp.zeros((n_blocks * tme,), jnp.int32).at[dest.reshape(-1)].set(tok)

    xbuf = jnp.take(h2, tok_of_row, axis=0)
    act = _expert_up(be, first, nu, xbuf, layer0(w_e_up), b_e_up.reshape(N_EXPERTS, 1, -1), tme, 1024)
    ybuf = _expert_down(be, first, nu, act, layer0(w_e_down), b_e_down.reshape(N_EXPERTS, 1, -1), tme, 1024)
    y = jnp.sum(jnp.take(ybuf, dest.reshape(-1), axis=0).reshape(n_tok, TOP_K, d) * gate[:, :, None], axis=1)
    out = x1 + y

    y_prompt = out[:n_prompt].reshape(n_batch, seq, d)
    y_sample = out[n_prompt:].reshape(n_db, n_new, d)
    k_meta = jnp.broadcast_to(qk[row_m:row_m + N_META, W_A:][None], (n_batch, N_META, W_A))
    v_meta = jnp.broadcast_to(z2[row_m:row_m + N_META, :W_A][None], (n_batch, N_META, W_A))
    k_prompt = jnp.concatenate([k_meta, qk[:n_prompt, W_A:].reshape(n_batch, seq, W_A)], axis=1)
    v_prompt = jnp.concatenate([v_meta, z2[:n_prompt, :W_A].reshape(n_batch, seq, W_A)], axis=1)
    k_prompt = k_prompt.reshape(1, n_batch, seq + N_META, H_A, VH_A)
    v_prompt = v_prompt.reshape(1, n_batch, seq + N_META, H_A, VH_A)
    k_sample = qk_s[:, W_A:].reshape(1, n_db, n_new, H_A, VH_A)
    v_sample = z2_s[:, :W_A].reshape(1, n_db, n_new, H_A, VH_A)
    return (y_prompt, y_sample, k_prompt, v_prompt, s_prompt[None], k_sample, v_sample, s_sample[None])
```

```python
import functools
import math

import jax
import jax.numpy as jnp
from jax import lax
from jax.experimental import pallas as pl
from jax.experimental.pallas import tpu as pltpu

F32 = jnp.float32
BF16 = jnp.bfloat16
HI = lax.Precision.HIGHEST

N_META = 16
H_A = 8
DH_A = 64
VH_A = 2 * DH_A
W_A = H_A * VH_A
H_B = 4
DK_B = 128
DV_B = 256
W_KB = H_B * DK_B
W_B = H_B * DV_B
GLA_RANK = 16
GLA_TAU = 16.0
GLA_CHUNK = 64
GLA_SUB = 16
N_EXPERTS = 32
TOP_K = 4
SWIGLU_LIMIT = 7.0
SWIGLU_ALPHA = 1.702
PAGE_SIZE = 128
EPS = 1e-6
LAM_INIT = 0.8 - 0.6 * math.exp(-0.3 * 0)

COL_QK = 0
COL_Z2 = 2 * W_A
Z2_COLS = W_A + 2 * W_KB + 2 * W_B
COL_A = COL_Z2 + Z2_COLS
COL_G = COL_A + GLA_RANK

LANES = 128
VMEM_LIMIT = 56 * 1024 * 1024

NEG = -1e30


def _mm(a, b, hi=False):
    if hi:
        return jnp.dot(a, b, precision=HI, preferred_element_type=F32)
    return jnp.dot(a.astype(BF16), b.astype(BF16), preferred_element_type=F32)


def _mm_nt(a, b, hi=False):
    dn = (((1,), (1,)), ((), ()))
    if hi:
        return lax.dot_general(a, b, dn, precision=HI, preferred_element_type=F32)
    return lax.dot_general(a.astype(BF16), b.astype(BF16), dn, preferred_element_type=F32)


def _mm_tn(a, b, hi=False):
    dn = (((0,), (0,)), ((), ()))
    if hi:
        return lax.dot_general(a, b, dn, precision=HI, preferred_element_type=F32)
    return lax.dot_general(a.astype(BF16), b.astype(BF16), dn, preferred_element_type=F32)


def _params(sem, vmem=VMEM_LIMIT):
    return pltpu.CompilerParams(dimension_semantics=sem, vmem_limit_bytes=vmem)


def _rms_kernel(x_ref, g_ref, o_ref):
    x = x_ref[...]
    ms = jnp.mean(x * x, axis=-1, keepdims=True)
    o_ref[...] = (x * lax.rsqrt(ms + EPS) * g_ref[...]).astype(o_ref.dtype)


def _rms(x, g, out_dtype, tm):
    m, d = x.shape
    return pl.pallas_call(
        _rms_kernel,
        grid=(m // tm,),
        in_specs=[pl.BlockSpec((tm, d), lambda i: (i, 0)),
                  pl.BlockSpec((1, d), lambda i: (0, 0))],
        out_specs=pl.BlockSpec((tm, d), lambda i: (i, 0)),
        out_shape=jax.ShapeDtypeStruct((m, d), out_dtype),
        compiler_params=_params(("arbitrary",)),
        name="rms_rows",
    )(x, g.reshape(1, d))


def _proj_kernel(*refs, mode, hi, tn):
    h_ref, w_ref = refs[0], refs[1]
    if mode == "qk":
        g_ref, o_ref = refs[2], refs[3]
        scr = refs[4:]
    elif mode == "loga":
        wfg_ref, bfg_ref, o_ref = refs[2], refs[3], refs[4]
        scr = refs[5:]
    else:
        o_ref = refs[2]
        scr = refs[3:]

    if hi:
        z = jnp.dot(h_ref[...], w_ref[...], precision=HI, preferred_element_type=F32)
    else:
        wbf = scr[0]

        @pl.when(pl.program_id(1) == 0)
        def _cast():
            wbf[...] = w_ref[...].astype(BF16)

        z = jnp.dot(h_ref[...], wbf[...], preferred_element_type=F32)

    if mode == "plain":
        o_ref[...] = z
    elif mode == "gate":
        o_ref[...] = jax.nn.sigmoid(z)
    elif mode == "qk":
        lane = lax.broadcasted_iota(jnp.int32, (z.shape[0], LANES), 1)
        lo = lane < DH_A
        g = g_ref[0]
        for c in range(tn // LANES):
            x = z[:, c * LANES:(c + 1) * LANES]
            x2 = x * x
            s0 = jnp.sum(jnp.where(lo, x2, 0.0), axis=-1, keepdims=True)
            s1 = jnp.sum(jnp.where(lo, 0.0, x2), axis=-1, keepdims=True)
            inv = jnp.where(lo, lax.rsqrt(s0 * (1.0 / DH_A) + EPS),
                            lax.rsqrt(s1 * (1.0 / DH_A) + EPS))
            o_ref[:, c * LANES:(c + 1) * LANES] = x * inv * g
    else:
        lane = lax.broadcasted_iota(jnp.int32, z.shape, 1)
        a = jnp.where(lane < GLA_RANK, z, 0.0)
        u = _mm(a, wfg_ref[...], hi) + bfg_ref[...]
        o_ref[...] = -(jnp.maximum(-u, 0.0) + jnp.log1p(jnp.exp(-jnp.abs(u)))) * (1.0 / GLA_TAU)


def _proj(h, w, col0, n_cols, tn, tm, mode, hi, extras=(), name="proj"):
    m, k = h.shape
    assert col0 % tn == 0 and n_cols % tn == 0 and m % tm == 0
    cb0 = col0 // tn
    if w.ndim == 3:
        w_spec = pl.BlockSpec((None, k, tn), lambda j, i: (0, 0, cb0 + j))
    else:
        w_spec = pl.BlockSpec((k, tn), lambda j, i: (0, cb0 + j))
    in_specs = [pl.BlockSpec((tm, k), lambda j, i: (i, 0)), w_spec]
    out_cols, out_tn = n_cols, tn
    if mode == "qk":
        in_specs.append(pl.BlockSpec((1, 1, LANES), lambda j, i: ((j * tn) // W_A, 0, 0)))
    elif mode == "loga":
        in_specs += [pl.BlockSpec((LANES, W_KB), lambda j, i: (0, 0)),
                     pl.BlockSpec((1, W_KB), lambda j, i: (0, 0))]
        out_cols, out_tn = W_KB, W_KB
    return pl.pallas_call(
        functools.partial(_proj_kernel, mode=mode, hi=hi, tn=tn),
        grid=(n_cols // tn, m // tm),
        in_specs=in_specs,
        out_specs=pl.BlockSpec((tm, out_tn), lambda j, i: (i, j)),
        out_shape=jax.ShapeDtypeStruct((m, out_cols), F32),
        scratch_shapes=[] if hi else [pltpu.VMEM((k, tn), BF16)],
        compiler_params=_params(("arbitrary", "arbitrary")),
        name=name,
    )(h, w, *extras)


def _attn_kernel(sc_ref, q_ref, k_ref, v_ref, km_ref, vm_ref, gs_ref, o_ref, kbf, vbf, m_s, l_s, acc_s, *,
                 tq, tk, n_heads):
    hg = pl.program_id(1)
    i = pl.program_id(2)
    half = tk // 2
    assert half == tq

    @pl.when(i == 0)
    def _cast():
        kbf[...] = k_ref[...].astype(BF16)
        vbf[...] = v_ref[...].astype(BF16)

    lam = sc_ref[0]
    q0 = i * tq
    width = 2 * tq
    lane = lax.broadcasted_iota(jnp.int32, (tq, LANES), 1)
    lo = lane < DH_A

    def tile_rel(n_keys):
        key = lax.broadcasted_iota(jnp.int32, (n_keys, width), 0)
        qry = lax.broadcasted_iota(jnp.int32, (n_keys, width), 1) % tq
        return key, qry, (qry - key).astype(F32)

    _, _, rel_full = tile_rel(tk)
    key, qry, rel_half = tile_rel(tq)

    def head_cols(g):
        return slice(g * LANES, (g + 1) * LANES)

    slopes, q12s = [], []
    for g in range(n_heads):
        slopes.append(sc_ref[1 + hg * n_heads + g])
        q = q_ref[:, head_cols(g)] * (DH_A ** -0.5)
        q12s.append(jnp.concatenate([jnp.where(lo, q, 0.0), jnp.where(lo, 0.0, q)], axis=0).astype(BF16))

    def scores(g, kc, n_keys, k0):
        s = lax.dot_general(kc, q12s[g], (((1,), (1,)), ((), ())), preferred_element_type=F32)
        rel = rel_full if n_keys == tk else rel_half
        return s + (rel + (q0 - k0).astype(F32)) * (-slopes[g])

    def update(g, s, vc):
        m = m_s[g]
        m_new = jnp.maximum(m, jnp.max(s, axis=0, keepdims=True))
        alpha = jnp.exp(m - m_new)
        p = jnp.exp(s - m_new)
        l_s[g] = alpha * l_s[g] + jnp.sum(p, axis=0, keepdims=True)
        acc_s[g] = alpha * acc_s[g] + _mm_tn(vc, p)
        m_s[g] = m_new

    for g in range(n_heads):
        s = scores(g, km_ref[:, head_cols(g)].astype(BF16), tq, -N_META)
        s = jnp.where(key < N_META, s, NEG)
        m0 = jnp.max(s, axis=0, keepdims=True)
        p = jnp.exp(s - m0)
        m_s[g] = m0
        l_s[g] = jnp.sum(p, axis=0, keepdims=True)
        acc_s[g] = _mm_tn(vm_ref[:, head_cols(g)], p)

    def full_chunk(j, carry):
        k0 = pl.multiple_of(j * tk, tk)
        for g in range(n_heads):
            update(g, scores(g, kbf[pl.ds(k0, tk), head_cols(g)], tk, k0), vbf[pl.ds(k0, tk), head_cols(g)])
        return carry

    def half_chunk(j, carry):
        k0 = pl.multiple_of(q0 - half, half)
        for g in range(n_heads):
            update(g, scores(g, kbf[pl.ds(k0, half), head_cols(g)], half, k0),
                   vbf[pl.ds(k0, half), head_cols(g)])
        return carry

    lax.fori_loop(0, q0 // tk, full_chunk, 0)
    lax.fori_loop(0, i % 2, half_chunk, 0)
    kq = pl.multiple_of(q0, tq)
    for g in range(n_heads):
        s = scores(g, kbf[pl.ds(kq, tq), head_cols(g)], tq, kq)
        s = jnp.where(qry >= key, s, NEG)
        update(g, s, vbf[pl.ds(kq, tq), head_cols(g)])
        accn = acc_s[g] / l_s[g]
        o = accn[:, :tq] - lam * accn[:, tq:]
        ms = jnp.mean(o * o, axis=0, keepdims=True)
        y = o * lax.rsqrt(ms + EPS) * gs_ref[...] * (1.0 - LAM_INIT)
        o_ref[:, head_cols(g)] = y.T.astype(o_ref.dtype)


def _prompt_attention(sc, qk, z2, g_sub, n_batch, seq, meta_row0, tq=128, tk=256, n_heads=8):
    nq = seq // tq
    n_prompt = n_batch * seq
    assert meta_row0 % tq == 0 and qk.shape[0] >= meta_row0 + tq and H_A % n_heads == 0
    meta_blk = meta_row0 // tq
    gw = n_heads * LANES
    kcol0 = W_A // gw
    return pl.pallas_call(
        functools.partial(_attn_kernel, tq=tq, tk=tk, n_heads=n_heads),
        grid=(n_batch, H_A // n_heads, nq),
        in_specs=[
            pl.BlockSpec(memory_space=pltpu.SMEM),
            pl.BlockSpec((tq, gw), lambda b, h, i: (b * nq + i, h)),
            pl.BlockSpec((seq, gw), lambda b, h, i: (b, kcol0 + h)),
            pl.BlockSpec((seq, gw), lambda b, h, i: (b, h)),
            pl.BlockSpec((tq, gw), lambda b, h, i: (meta_blk, kcol0 + h)),
            pl.BlockSpec((tq, gw), lambda b, h, i: (meta_blk, h)),
            pl.BlockSpec((VH_A, 1), lambda b, h, i: (0, 0)),
        ],
        out_specs=pl.BlockSpec((tq, gw), lambda b, h, i: (b * nq + i, h)),
        out_shape=jax.ShapeDtypeStruct((n_prompt, W_A), BF16),
        scratch_shapes=[pltpu.VMEM((seq, gw), BF16), pltpu.VMEM((seq, gw), BF16),
                        pltpu.VMEM((n_heads, 1, 2 * tq), F32), pltpu.VMEM((n_heads, 1, 2 * tq), F32),
                        pltpu.VMEM((n_heads, VH_A, 2 * tq), F32)],
        compiler_params=_params(("arbitrary", "arbitrary", "arbitrary")),
        name="prompt_attention",
    )(sc, qk, qk, z2, qk, z2, g_sub.reshape(VH_A, 1))


def _dec_kernel(pt_ref, sc_ref, qbd_ref, kn_ref, vn_ref, slc_ref, tcol_ref, gs_ref, *rest,
                g_pages, n_steps, past, n_new):
    kp_refs = rest[:g_pages]
    vp_refs = rest[g_pages:2 * g_pages]
    o_ref = rest[2 * g_pages]
    m_s, l_s, acc_s = rest[2 * g_pages + 1:]
    j = pl.program_id(1)

    @pl.when(j == 0)
    def _init():
        m_s[...] = jnp.full(m_s.shape, NEG, F32)
        l_s[...] = jnp.zeros(l_s.shape, F32)
        acc_s[...] = jnp.zeros(acc_s.shape, F32)

    qcat = qbd_ref[...]
    slc = slc_ref[...]
    tcol = tcol_ref[...]
    keyrow = lax.broadcasted_iota(jnp.int32, (PAGE_SIZE, LANES), 0).astype(F32)
    rel = tcol - keyrow

    def split(x):
        hi = x.astype(BF16)
        r = x - hi.astype(F32)
        mid = r.astype(BF16)
        return hi, mid, (r - mid.astype(F32)).astype(BF16)

    def update(s_all, pv_fn):
        m_old = m_s[...]
        m_new = jnp.maximum(m_old, jnp.max(s_all, axis=0, keepdims=True))
        alpha = jnp.exp(m_old - m_new)
        p = jnp.exp(s_all - m_new)
        l_s[...] = alpha * l_s[...] + jnp.sum(p, axis=0, keepdims=True)
        acc_s[...] = acc_s[...] * alpha + pv_fn(p)
        m_s[...] = m_new

    def halves(x):
        return x[:, :LANES], x[:, LANES:]

    q_hm, q_hi, q_lo = qcat[:, :2 * LANES], qcat[:, :LANES], qcat[:, 2 * LANES:]
    ss = []
    for u in range(g_pages):
        page0 = (past - (j * g_pages + u) * PAGE_SIZE).astype(F32)
        k_hi, k_mid, k_lo = split(kp_refs[u][...])
        hh, hm = halves(jnp.dot(k_hi, q_hm, preferred_element_type=F32))
        mh, mm = halves(jnp.dot(k_mid, q_hm, preferred_element_type=F32))
        small = (mm + jnp.dot(k_hi, q_lo, preferred_element_type=F32)
                 + jnp.dot(k_lo, q_hi, preferred_element_type=F32))
        s = hh + (hm + mh + small)
        ss.append(s + (rel + page0) * (-slc))
    s_all = jnp.concatenate(ss, axis=0)

    def pv_pages(p):
        p_hi, p_mid, p_lo = split(p)
        p_hm = jnp.concatenate([p_hi, p_mid], axis=1)
        out = None
        for u in range(g_pages):
            rows = slice(u * PAGE_SIZE, (u + 1) * PAGE_SIZE)
            v_hi, v_mid, v_lo = split(vp_refs[u][...])
            hh, hm = halves(_mm_tn(v_hi, p_hm[rows]))
            mh, mm = halves(_mm_tn(v_mid, p_hm[rows]))
            small = mm + _mm_tn(v_hi, p_lo[rows]) + _mm_tn(v_lo, p_hi[rows])
            t = hh + (hm + mh + small)
            out = t if out is None else out + t
        return out

    update(s_all, pv_pages)

    @pl.when(j == n_steps - 1)
    def _finish():
        qbd = q_hi.astype(F32) + qcat[:, LANES:2 * LANES].astype(F32) + q_lo.astype(F32)
        trow = lax.broadcasted_iota(jnp.int32, (n_new, LANES), 0).astype(F32)
        s = jnp.dot(kn_ref[...], qbd, precision=HI, preferred_element_type=F32)
        s = s + (tcol - trow) * (-slc)
        s = jnp.where(trow <= tcol, s, NEG)
        update(s, lambda p: _mm_tn(vn_ref[...], p, True))

        lam = sc_ref[0]
        accn = acc_s[...] / l_s[...]
        for h in range(H_A):
            blk = accn[h * VH_A:(h + 1) * VH_A, h * 2 * n_new:(h + 1) * 2 * n_new]
            d = blk[:, :n_new] - lam * blk[:, n_new:]
            ms = jnp.mean(d * d, axis=0, keepdims=True)
            o_ref[h * VH_A:(h + 1) * VH_A, :] = d * lax.rsqrt(ms + EPS) * gs_ref[...] * (1.0 - LAM_INIT)


def _decode_attention(page_table, sc, qbd, qk_s, z2_s, cache_k, cache_v, slc, tcol, g_sub, g_pages=8):
    n_db, n_pages = page_table.shape
    n_new = qk_s.shape[0] // n_db
    past = n_pages * PAGE_SIZE
    n_steps = n_pages // g_pages
    n_pool = cache_k.shape[-4]
    width = H_A * VH_A
    ck = cache_k.reshape(n_pool, PAGE_SIZE, width)
    cv = cache_v.reshape(n_pool, PAGE_SIZE, width)
    kcol0 = W_A // width
    q_hi = qbd.astype(BF16)
    q_res = qbd - q_hi.astype(F32)
    q_mid = q_res.astype(BF16)
    qcat = jnp.concatenate([q_hi, q_mid, (q_res - q_mid.astype(F32)).astype(BF16)], axis=-1)

    def page_spec(u):
        return pl.BlockSpec((None, PAGE_SIZE, width),
                            lambda d, j, pt: (pt[d * n_pages + j * g_pages + u], 0, 0))

    in_specs = [
        pl.BlockSpec(memory_space=pltpu.SMEM),
        pl.BlockSpec((None, width, 3 * LANES), lambda d, j, pt: (d, 0, 0)),
        pl.BlockSpec((n_new, width), lambda d, j, pt: (d, kcol0)),
        pl.BlockSpec((n_new, width), lambda d, j, pt: (d, 0)),
        pl.BlockSpec((1, LANES), lambda d, j, pt: (0, 0)),
        pl.BlockSpec((1, LANES), lambda d, j, pt: (0, 0)),
        pl.BlockSpec((VH_A, 1), lambda d, j, pt: (0, 0)),
    ] + [page_spec(u) for u in range(g_pages)] * 2
    grid_spec = pltpu.PrefetchScalarGridSpec(
        num_scalar_prefetch=1,
        grid=(n_db, n_steps),
        in_specs=in_specs,
        out_specs=pl.BlockSpec((None, width, n_new), lambda d, j, pt: (d, 0, 0)),
        scratch_shapes=[pltpu.VMEM((1, LANES), F32), pltpu.VMEM((1, LANES), F32),
                        pltpu.VMEM((width, LANES), F32)],
    )
    return pl.pallas_call(
        functools.partial(_dec_kernel, g_pages=g_pages, n_steps=n_steps, past=past, n_new=n_new),
        grid_spec=grid_spec,
        out_shape=jax.ShapeDtypeStruct((n_db, width, n_new), F32),
        compiler_params=_params(("arbitrary", "arbitrary")),
        name="decode_attention",
    )(page_table.reshape(-1), sc, qcat, qk_s, z2_s, slc, tcol, g_sub.reshape(VH_A, 1),
      *([ck] * g_pages), *([cv] * g_pages))


def _gla_kernel(q_ref, k_ref, v_ref, la_ref, r_ref, s0_ref, gg_ref, o_ref, so_ref, state, *,
                chunk, sub, n_chunks, hi):
    state[...] = s0_ref[...]
    row = lax.broadcasted_iota(jnp.int32, (chunk, chunk), 0)
    col = lax.broadcasted_iota(jnp.int32, (chunk, chunk), 1)
    tri = (col <= row).astype(F32)
    tri_sub = (col <= (row // sub) * sub + (sub - 1)).astype(F32)
    causal = col <= row
    colblk = col // sub
    rowc = lax.broadcasted_iota(jnp.int32, (chunk, DK_B), 0)
    ones = jnp.ones((chunk, DK_B), F32)
    n_sub = chunk // sub

    def body(c, carry):
        c0 = pl.multiple_of(c * chunk, chunk)
        sl = pl.ds(c0, chunk)
        la = la_ref[sl, :]
        b = jnp.dot(tri, la, precision=HI, preferred_element_type=F32)
        rsub = jnp.dot(tri_sub, la, precision=HI, preferred_element_type=F32)
        b_end = b[chunk - 1:chunk, :]
        qs = q_ref[sl, :] * (DK_B ** -0.5)
        kk = k_ref[sl, :]
        vv = v_ref[sl, :]
        s_old = state[...]

        inter = _mm(qs * jnp.exp(b), s_old, hi)

        kt = kk * jnp.exp(rsub - b)
        qj = []
        for jb in range(n_sub):
            ref = rsub[jb * sub:jb * sub + 1, :]
            qj.append(qs * jnp.exp(jnp.where(rowc >= jb * sub, b - ref, -jnp.inf)))
        a4 = _mm_nt(jnp.concatenate(qj, axis=0), kt, hi)
        att = a4[:chunk]
        for jb in range(1, n_sub):
            att = jnp.where(colblk == jb, a4[jb * chunk:(jb + 1) * chunk], att)
        att = jnp.where(causal, att, 0.0)
        o = inter + _mm(att, vv, hi)

        dec = jnp.exp(_mm_tn(la, ones, True))
        kd = kk * jnp.exp(b_end - b)
        state[...] = jnp.concatenate([dec] * (DV_B // DK_B), axis=1) * s_old + _mm_tn(kd, vv, hi)

        ms = jnp.mean(o * o, axis=-1, keepdims=True)
        r = r_ref[sl, :]
        y = o * lax.rsqrt(ms + EPS) * gg_ref[...] * (r * jax.nn.sigmoid(r))
        o_ref[sl, :] = y.astype(o_ref.dtype)
        return carry

    lax.fori_loop(0, n_chunks, body, 0)
    so_ref[...] = state[...]


def _gla(z2, la, s0, g_gla, n_batch, seq, row0, chunk, sub, hi, out_dtype, name):
    rb0 = row0 // seq
    qc0 = W_A // DK_B
    kc0 = (W_A + W_KB) // DK_B
    vc0 = (W_A + 2 * W_KB) // DV_B
    rc0 = (W_A + 2 * W_KB + W_B) // DV_B
    per_batch_state = s0.shape[0] != 1
    return pl.pallas_call(
        functools.partial(_gla_kernel, chunk=chunk, sub=sub, n_chunks=seq // chunk, hi=hi),
        grid=(n_batch, H_B),
        in_specs=[
            pl.BlockSpec((seq, DK_B), lambda b, h: (rb0 + b, qc0 + h)),
            pl.BlockSpec((seq, DK_B), lambda b, h: (rb0 + b, kc0 + h)),
            pl.BlockSpec((seq, DV_B), lambda b, h: (rb0 + b, vc0 + h)),
            pl.BlockSpec((seq, DK_B), lambda b, h: (rb0 + b, h)),
            pl.BlockSpec((seq, DV_B), lambda b, h: (rb0 + b, rc0 + h)),
            pl.BlockSpec((None, None, DK_B, DV_B),
                         (lambda b, h: (b, h, 0, 0)) if per_batch_state else (lambda b, h: (0, h, 0, 0))),
            pl.BlockSpec((1, DV_B), lambda b, h: (0, 0)),
        ],
        out_specs=[
            pl.BlockSpec((seq, DV_B), lambda b, h: (b, h)),
            pl.BlockSpec((None, None, DK_B, DV_B), lambda b, h: (b, h, 0, 0)),
        ],
        out_shape=[jax.ShapeDtypeStruct((n_batch * seq, W_B), out_dtype),
                   jax.ShapeDtypeStruct((n_batch, H_B, DK_B, DV_B), F32)],
        scratch_shapes=[pltpu.VMEM((DK_B, DV_B), F32)],
        compiler_params=_params(("arbitrary", "arbitrary")),
        name=name,
    )(z2, z2, z2, la, z2, s0, g_gla.reshape(1, DV_B))


def _merge_kernel(oa_ref, og_ref, h_ref, wa_ref, wb_ref, wga_ref, wgb_ref, o_ref):
    h = h_ref[...]
    pa = jnp.dot(oa_ref[...], wa_ref[...], preferred_element_type=F32)
    ga = jax.nn.sigmoid(jnp.dot(h, wga_ref[...], preferred_element_type=F32))
    acc = ga * pa
    pb = jnp.dot(og_ref[...], wb_ref[...], preferred_element_type=F32)
    gb = jax.nn.sigmoid(jnp.dot(h, wgb_ref[...], preferred_element_type=F32))
    o_ref[...] = (acc + gb * pb).astype(o_ref.dtype)


def _merge(oa, og, h, wa, wb, wg, tm, tn):
    m, k = h.shape
    d = wa.shape[1]
    gb0 = d // tn
    return pl.pallas_call(
        _merge_kernel,
        grid=(d // tn, m // tm),
        in_specs=[
            pl.BlockSpec((tm, W_A), lambda j, i: (i, 0)),
            pl.BlockSpec((tm, W_B), lambda j, i: (i, 0)),
            pl.BlockSpec((tm, k), lambda j, i: (i, 0)),
            pl.BlockSpec((W_A, tn), lambda j, i: (0, j)),
            pl.BlockSpec((W_B, tn), lambda j, i: (0, j)),
            pl.BlockSpec((k, tn), lambda j, i: (0, j)),
            pl.BlockSpec((k, tn), lambda j, i: (0, gb0 + j)),
        ],
        out_specs=pl.BlockSpec((tm, tn), lambda j, i: (i, j)),
        out_shape=jax.ShapeDtypeStruct((m, d), BF16),
        compiler_params=_params(("arbitrary", "arbitrary")),
        name="merge",
    )(oa, og, h, wa, wb, wg, wg)


def _merge_s_kernel(oa_ref, og_ref, wa_ref, wb_ref, ga_ref, gb_ref, o_ref):
    pa = _mm(oa_ref[...], wa_ref[...], True)
    pb = _mm(og_ref[...], wb_ref[...], True)
    o_ref[...] = ga_ref[...] * pa + gb_ref[...] * pb


def _merge_sample(oa, og, wa, wb, sg, tn):
    m = oa.shape[0]
    d = wa.shape[1]
    gb0 = d // tn
    return pl.pallas_call(
        _merge_s_kernel,
        grid=(d // tn,),
        in_specs=[
            pl.BlockSpec((m, W_A), lambda j: (0, 0)),
            pl.BlockSpec((m, W_B), lambda j: (0, 0)),
            pl.BlockSpec((W_A, tn), lambda j: (0, j)),
            pl.BlockSpec((W_B, tn), lambda j: (0, j)),
            pl.BlockSpec((m, tn), lambda j: (0, j)),
            pl.BlockSpec((m, tn), lambda j: (0, gb0 + j)),
        ],
        out_specs=pl.BlockSpec((m, tn), lambda j: (0, j)),
        out_shape=jax.ShapeDtypeStruct((m, d), F32),
        compiler_params=_params(("arbitrary",)),
        name="merge_sample",
    )(oa, og, wa, wb, sg, sg)


def _outproj_kernel(mg_ref, w_ref, x_ref, gf_ref, x1_ref, h2_ref, *, hi):
    x1 = x_ref[...] + _mm(mg_ref[...], w_ref[...], hi)
    x1_ref[...] = x1
    ms = jnp.mean(x1 * x1, axis=-1, keepdims=True)
    h2_ref[...] = (x1 * lax.rsqrt(ms + EPS) * gf_ref[...]).astype(h2_ref.dtype)


def _outproj(mg, w, x, g_ffn, tm, hi, name):
    m, d = x.shape
    return pl.pallas_call(
        functools.partial(_outproj_kernel, hi=hi),
        grid=(m // tm,),
        in_specs=[
            pl.BlockSpec((tm, d), lambda i: (i, 0)),
            pl.BlockSpec((d, d), lambda i: (0, 0)),
            pl.BlockSpec((tm, d), lambda i: (i, 0)),
            pl.BlockSpec((1, d), lambda i: (0, 0)),
        ],
        out_specs=[pl.BlockSpec((tm, d), lambda i: (i, 0)), pl.BlockSpec((tm, d), lambda i: (i, 0))],
        out_shape=[jax.ShapeDtypeStruct((m, d), F32), jax.ShapeDtypeStruct((m, d), F32)],
        compiler_params=_params(("arbitrary",)),
        name=name,
    )(mg, w, x, g_ffn.reshape(1, d))


def _route_kernel(h_ref, w_ref, b_ref, c0_ref, idx_ref, gate_ref, rank_ref, cnt_ref, carry, *, tm, hi):
    @pl.when(pl.program_id(0) == 0)
    def _init():
        carry[...] = c0_ref[...]

    logits = _mm(h_ref[...], w_ref[...], hi) + b_ref[...]
    lane = lax.broadcasted_iota(jnp.int32, (tm, N_EXPERTS), 1).astype(F32)
    work = logits
    vals, idxs, sels = [], [], []
    for _ in range(TOP_K):
        mx = jnp.max(work, axis=-1, keepdims=True)
        ix = jnp.min(jnp.where(work == mx, lane, float(N_EXPERTS)), axis=-1, keepdims=True)
        sel = lane == ix
        vals.append(mx)
        idxs.append(ix)
        sels.append(sel)
        work = jnp.where(sel, -jnp.inf, work)
    es = [jnp.exp(v - vals[0]) for v in vals]
    tot = es[0] + es[1] + es[2] + es[3]
    gate_ref[...] = jnp.concatenate([e / tot for e in es], axis=1)
    idx_ref[...] = jnp.concatenate(idxs, axis=1).astype(jnp.int32)

    onehot = jnp.zeros((tm, N_EXPERTS), F32)
    for sel in sels:
        onehot = onehot + sel.astype(F32)
    r = lax.broadcasted_iota(jnp.int32, (tm, tm), 0)
    c = lax.broadcasted_iota(jnp.int32, (tm, tm), 1)
    before = (c < r).astype(BF16)
    cum = jnp.dot(before, onehot.astype(BF16), preferred_element_type=F32) + carry[...]
    ranks = [jnp.sum(jnp.where(sel, cum, 0.0), axis=-1, keepdims=True) for sel in sels]
    rank_ref[...] = jnp.concatenate(ranks, axis=1).astype(jnp.int32)
    carry[...] = carry[...] + jnp.sum(onehot, axis=0, keepdims=True)
    cnt_ref[...] = carry[...]


def _route(h, n_tok, w_router, b_router, c0, tm, hi, name):
    d = h.shape[1]
    assert n_tok % tm == 0
    return pl.pallas_call(
        functools.partial(_route_kernel, tm=tm, hi=hi),
        grid=(n_tok // tm,),
        in_specs=[
            pl.BlockSpec((tm, d), lambda i: (i, 0)),
            pl.BlockSpec((d, N_EXPERTS), lambda i: (0, 0)),
            pl.BlockSpec((1, N_EXPERTS), lambda i: (0, 0)),
            pl.BlockSpec((1, N_EXPERTS), lambda i: (0, 0)),
        ],
        out_specs=[
            pl.BlockSpec((tm, TOP_K), lambda i: (i, 0)),
            pl.BlockSpec((tm, TOP_K), lambda i: (i, 0)),
            pl.BlockSpec((tm, TOP_K), lambda i: (i, 0)),
            pl.BlockSpec((1, N_EXPERTS), lambda i: (0, 0)),
        ],
        out_shape=[
            jax.ShapeDtypeStruct((n_tok, TOP_K), jnp.int32),
            jax.ShapeDtypeStruct((n_tok, TOP_K), F32),
            jax.ShapeDtypeStruct((n_tok, TOP_K), jnp.int32),
            jax.ShapeDtypeStruct((1, N_EXPERTS), F32),
        ],
        scratch_shapes=[pltpu.VMEM((1, N_EXPERTS), F32)],
        compiler_params=_params(("arbitrary",)),
        name=name,
    )(h, w_router, b_router.reshape(1, N_EXPERTS), c0)


def _up_kernel(be_ref, first_ref, nu_ref, x_ref, wg_ref, wl_ref, bg_ref, bl_ref, o_ref, wgb, wlb):
    b = pl.program_id(1)

    @pl.when(first_ref[b] == 1)
    def _cast():
        wgb[...] = wg_ref[...].astype(BF16)
        wlb[...] = wl_ref[...].astype(BF16)

    @pl.when(b < nu_ref[0])
    def _compute():
        x = x_ref[...]
        g = jnp.dot(x, wgb[...], preferred_element_type=F32) + bg_ref[...]
        lin = jnp.dot(x, wlb[...], preferred_element_type=F32) + bl_ref[...]
        g = jnp.minimum(g, SWIGLU_LIMIT)
        lin = jnp.clip(lin, -SWIGLU_LIMIT, SWIGLU_LIMIT)
        glu = g * jax.nn.sigmoid(SWIGLU_ALPHA * g)
        o_ref[...] = ((lin + 1.0) * glu).astype(o_ref.dtype)

    @pl.when(b >= nu_ref[0])
    def _unused():
        o_ref[...] = jnp.zeros(o_ref.shape, o_ref.dtype)


def _expert_up(be, first, nu, xbuf, w_up, b_up, tm, th):
    rows, d = xbuf.shape
    d_e = w_up.shape[-1] // 2
    nj = d_e // th
    grid_spec = pltpu.PrefetchScalarGridSpec(
        num_scalar_prefetch=3,
        grid=(nj, rows // tm),
        in_specs=[
            pl.BlockSpec((tm, d), lambda j, b, be, fi, nu: (b, 0)),
            pl.BlockSpec((None, None, d, th), lambda j, b, be, fi, nu: (0, be[b], 0, j)),
            pl.BlockSpec((None, None, d, th), lambda j, b, be, fi, nu: (0, be[b], 0, nj + j)),
            pl.BlockSpec((None, None, 1, th), lambda j, b, be, fi, nu: (0, be[b], 0, j)),
            pl.BlockSpec((None, None, 1, th), lambda j, b, be, fi, nu: (0, be[b], 0, nj + j)),
        ],
        out_specs=pl.BlockSpec((tm, th), lambda j, b, be, fi, nu: (b, j)),
        scratch_shapes=[pltpu.VMEM((d, th), BF16), pltpu.VMEM((d, th), BF16)],
    )
    return pl.pallas_call(
        _up_kernel,
        grid_spec=grid_spec,
        out_shape=jax.ShapeDtypeStruct((rows, d_e), BF16),
        compiler_params=_params(("arbitrary", "arbitrary")),
        name="expert_up",
    )(be, first, nu, xbuf, w_up, w_up, b_up, b_up)


def _down_kernel(be_ref, first_ref, nu_ref, a_ref, w_ref, b_ref, o_ref, wb):
    b = pl.program_id(1)

    @pl.when(first_ref[b] == 1)
    def _cast():
        wb[...] = w_ref[...].astype(BF16)

    @pl.when(b < nu_ref[0])
    def _compute():
        o_ref[...] = jnp.dot(a_ref[...], wb[...], preferred_element_type=F32) + b_ref[...]

    @pl.when(b >= nu_ref[0])
    def _unused():
        o_ref[...] = jnp.zeros(o_ref.shape, o_ref.dtype)


def _expert_down(be, first, nu, act, w_down, b_down, tm, tn):
    rows, d_e = act.shape
    d = w_down.shape[-1]
    grid_spec = pltpu.PrefetchScalarGridSpec(
        num_scalar_prefetch=3,
        grid=(d // tn, rows // tm),
        in_specs=[
            pl.BlockSpec((tm, d_e), lambda j, b, be, fi, nu: (b, 0)),
            pl.BlockSpec((None, None, d_e, tn), lambda j, b, be, fi, nu: (0, be[b], 0, j)),
            pl.BlockSpec((None, None, 1, tn), lambda j, b, be, fi, nu: (0, be[b], 0, j)),
        ],
        out_specs=pl.BlockSpec((tm, tn), lambda j, b, be, fi, nu: (b, j)),
        scratch_shapes=[pltpu.VMEM((d_e, tn), BF16)],
    )
    return pl.pallas_call(
        _down_kernel,
        grid_spec=grid_spec,
        out_shape=jax.ShapeDtypeStruct((rows, d), F32),
        compiler_params=_params(("arbitrary", "arbitrary")),
        name="expert_down",
    )(be, first, nu, act, w_down, b_down)


def _row_copy(src_hbm, row, dst, sem):
    return pltpu.make_async_copy(src_hbm.at[pl.ds(row, 1)], dst, sem)


def _dispatch_kernel(tok_ref, h_hbm, o_ref, buf, sem, *, tm):
    b = pl.program_id(0)

    def start_tile(blk, slot):
        def body(r, carry):
            _row_copy(h_hbm, tok_ref[blk * tm + r], buf.at[slot, pl.ds(r, 1)], sem.at[slot]).start()
            return carry
        lax.fori_loop(0, tm, body, 0, unroll=8)

    @pl.when(b == 0)
    def _first():
        start_tile(0, 0)

    @pl.when(b + 1 < pl.num_programs(0))
    def _next():
        start_tile(b + 1, (b + 1) % 2)

    slot = b % 2

    def wait_body(r, carry):
        _row_copy(h_hbm, 0, buf.at[slot, pl.ds(r, 1)], sem.at[slot]).wait()
        return carry
    lax.fori_loop(0, tm, wait_body, 0, unroll=8)
    o_ref[...] = buf[slot].astype(o_ref.dtype)


def _dispatch(tok_of_row, h, tm):
    rows = tok_of_row.shape[0]
    d = h.shape[1]
    grid_spec = pltpu.PrefetchScalarGridSpec(
        num_scalar_prefetch=1,
        grid=(rows // tm,),
        in_specs=[pl.BlockSpec(memory_space=pl.ANY)],
        out_specs=pl.BlockSpec((tm, d), lambda b, tok: (b, 0)),
        scratch_shapes=[pltpu.VMEM((2, tm, d), F32), pltpu.SemaphoreType.DMA((2,))],
    )
    return pl.pallas_call(
        functools.partial(_dispatch_kernel, tm=tm),
        grid_spec=grid_spec,
        out_shape=jax.ShapeDtypeStruct((rows, d), BF16),
        compiler_params=_params(("arbitrary",)),
        name="moe_dispatch",
    )(tok_of_row, h)


def _combine_kernel(dest_ref, y_hbm, x_ref, gate_ref, o_ref, buf, sem, *, tm):
    i = pl.program_id(0)

    def start_tile(blk, slot):
        def body(r, carry):
            for k in range(TOP_K):
                row = dest_ref[(blk * tm + r) * TOP_K + k]
                _row_copy(y_hbm, row, buf.at[slot, k, pl.ds(r, 1)], sem.at[slot]).start()
            return carry
        lax.fori_loop(0, tm, body, 0, unroll=4)

    @pl.when(i == 0)
    def _first():
        start_tile(0, 0)

    @pl.when(i + 1 < pl.num_programs(0))
    def _next():
        start_tile(i + 1, (i + 1) % 2)

    slot = i % 2

    def wait_body(r, carry):
        for k in range(TOP_K):
            _row_copy(y_hbm, 0, buf.at[slot, k, pl.ds(r, 1)], sem.at[slot]).wait()
        return carry
    lax.fori_loop(0, tm, wait_body, 0, unroll=4)
    g = gate_ref[...]
    y = g[:, 0:1] * buf[slot, 0]
    for k in range(1, TOP_K):
        y = y + g[:, k:k + 1] * buf[slot, k]
    o_ref[...] = x_ref[...] + y


def _combine(dest, ybuf, x1, gate, tm):
    n_tok, d = x1.shape
    assert n_tok % tm == 0
    grid_spec = pltpu.PrefetchScalarGridSpec(
        num_scalar_prefetch=1,
        grid=(n_tok // tm,),
        in_specs=[pl.BlockSpec(memory_space=pl.ANY),
                  pl.BlockSpec((tm, d), lambda i, dest: (i, 0)),
                  pl.BlockSpec((tm, TOP_K), lambda i, dest: (i, 0))],
        out_specs=pl.BlockSpec((tm, d), lambda i, dest: (i, 0)),
        scratch_shapes=[pltpu.VMEM((2, TOP_K, tm, d), F32), pltpu.SemaphoreType.DMA((2,))],
    )
    return pl.pallas_call(
        functools.partial(_combine_kernel, tm=tm),
        grid_spec=grid_spec,
        out_shape=jax.ShapeDtypeStruct((n_tok, d), F32),
        compiler_params=_params(("arbitrary",)),
        name="moe_combine",
    )(dest.reshape(-1), ybuf, x1, gate)


def kernel(x_prompt, x_sample, cache_k, cache_v, state_gla, page_table, meta_tokens, g_mix, w_in, g_q, g_k, lam_q1, lam_k1, lam_q2, lam_k2, g_sub, w_fg, b_fg, g_gla, w_br_a, w_br_b, w_out, g_ffn, w_router, b_router, w_e_up, b_e_up, w_e_down, b_e_down):
    assert w_in.shape[0] == 1, "single-layer trunk"
    n_batch, seq, d = x_prompt.shape
    n_db, n_new, _ = x_sample.shape
    assert 2 * H_A * n_new == LANES
    n_prompt = n_batch * seq
    n_sample = n_db * n_new
    n_tok = n_prompt + n_sample
    row_m = n_prompt

    def layer0(x):
        return x.reshape(x.shape[1:])

    w = layer0(w_in)
    w_a, w_b, w_o, w_r = layer0(w_br_a), layer0(w_br_b), layer0(w_out), layer0(w_router)
    lam =(jnp.exp(jnp.sum(lam_q1[0] * lam_k1[0])) - jnp.exp(jnp.sum(lam_q2[0] * lam_k2[0])) + LAM_INIT)
    slopes = jnp.exp2(-8.0 * jnp.arange(1, H_A + 1, dtype=F32) / H_A)
    sc = jnp.concatenate([lam.reshape(1), slopes]).astype(F32)
    g_qk = jnp.stack([g_q[0].reshape(1, VH_A), g_k[0].reshape(1, VH_A)])
    wfg_pad = jnp.zeros((LANES, W_KB), F32).at[:GLA_RANK].set(w_fg[0])
    bfg = b_fg[0].reshape(1, W_KB)
    w_gates = w[:, COL_G:]

    pad_rows = (-(n_prompt + N_META)) % (5 * 13 * LANES)
    xm = jnp.concatenate([x_prompt.reshape(n_prompt, d), meta_tokens.astype(F32),
                          jnp.zeros((pad_rows, d), F32)], axis=0)
    n_rows = xm.shape[0]
    tm_proj = n_rows // 5
    tm_row = n_rows // 13
    hm = _rms(xm, g_mix[0], BF16, tm=tm_row)
    qk = _proj(hm, w_in, COL_QK, 2 * W_A, 512, tm_proj, "qk", False, (g_qk,), name="proj_qk")
    z2 = _proj(hm, w_in, COL_Z2, Z2_COLS, 512, tm_proj, "plain", False, name="proj_z2")
    la = _proj(hm, w_in, COL_A, LANES, LANES, tm_proj, "loga", False, (wfg_pad, bfg), name="proj_loga")

    oa_p = _prompt_attention(sc, qk, z2, g_sub[0], n_batch, seq, row_m)
    zero_state = jnp.zeros((1, H_B, DK_B, DV_B), F32)
    _, s_meta = _gla(z2, la, zero_state, g_gla[0], 1, N_META, row_m, N_META, N_META, False, BF16, "gla_meta")
    og_p, s_prompt = _gla(z2, la, s_meta, g_gla[0], n_batch, seq, 0, GLA_CHUNK, GLA_SUB, False, BF16,
                          "gla_prompt")
    tail = jnp.zeros((n_rows - n_prompt, W_A), BF16)
    mg = _merge(jnp.concatenate([oa_p, tail], axis=0), jnp.concatenate([og_p, tail], axis=0), hm,
                w_a.astype(BF16), w_b.astype(BF16), w_gates.astype(BF16), tm_row, 512)
    x1_m, h2_m = _outproj(mg, w_o.astype(BF16), xm, g_ffn[0], tm_row // 2, False, "outproj")

    xs = x_sample.reshape(n_sample, d)
    hs = _rms(xs, g_mix[0], F32, tm=n_sample)
    qk_s = _proj(hs, w_in, COL_QK, 2 * W_A, 512, n_sample, "qk", True, (g_qk,), name="proj_qk_s")
    z2_s = _proj(hs, w_in, COL_Z2, Z2_COLS, 512, n_sample, "plain", True, name="proj_z2_s")
    la_s = _proj(hs, w_in, COL_A, LANES, LANES, n_sample, "loga", True, (wfg_pad, bfg), name="proj_loga_s")
    sg_s = _proj(hs, w_gates, 0, 2 * d, 512, n_sample, "gate", True, name="proj_gate_s")

    q_s = (qk_s[:, :W_A] * (DH_A ** -0.5)).reshape(n_db, n_new, H_A, 2, DH_A)
    eye_h = jnp.eye(H_A, dtype=F32)
    eye_m = jnp.eye(2, dtype=F32)
    qbd = jnp.einsum("bthmd,hg,mn->bhmdgnt", q_s, eye_h, eye_m).reshape(n_db, W_A, LANES)
    col = jnp.arange(LANES)
    slc = slopes[col // (2 * n_new)].reshape(1, LANES)
    tcol = (col % n_new).astype(F32).reshape(1, LANES)
    oa_t = _decode_attention(page_table, sc, qbd, qk_s, z2_s, cache_k, cache_v, slc, tcol, g_sub[0])
    oa_s = jnp.swapaxes(oa_t, 1, 2).reshape(n_sample, W_A)
    og_s, s_sample = _gla(z2_s, la_s, layer0(state_gla), g_gla[0], n_db, n_new, 0, n_new, n_new, True, F32,
                          "gla_sample")
    mg_s = _merge_sample(oa_s, og_s, w_a, w_b, sg_s, 512)
    x1_s, h2_s = _outproj(mg_s, w_o, xs, g_ffn[0], n_sample, True, "outproj_sample")

    c0 = jnp.zeros((1, N_EXPERTS), F32)
    idx_m, gate_m, rank_m, cnt_m = _route(h2_m, n_prompt, w_r, b_router[0], c0, 512, False, "route_prompt")
    idx_s, gate_s, rank_s, cnt = _route(h2_s, n_sample, w_r, b_router[0], cnt_m, n_sample, True,
                                        "route_sample")
    eidx = jnp.concatenate([idx_m, idx_s], axis=0)
    gate = jnp.concatenate([gate_m, gate_s], axis=0)
    rank = jnp.concatenate([rank_m, rank_s], axis=0)
    h2 = jnp.concatenate([h2_m[:n_prompt], h2_s], axis=0)
    x1 = jnp.concatenate([x1_m[:n_prompt], x1_s], axis=0)

    tme = 256
    n_blocks = -(-(n_tok * TOP_K) // tme) + N_EXPERTS
    counts = cnt.reshape(N_EXPERTS).astype(jnp.int32)
    padded = (counts + tme - 1) // tme * tme
    pend = jnp.cumsum(padded)
    pstart = pend - padded
    dest = pstart[eidx] + rank
    n_used = pend[-1] // tme
    blk = jnp.arange(n_blocks, dtype=jnp.int32)
    be = jnp.minimum(jnp.sum(pend[None, :] <= (blk * tme)[:, None], axis=1), N_EXPERTS - 1).astype(jnp.int32)
    be = jnp.where(blk < n_used, be, be[jnp.maximum(n_used - 1, 0)])
    first = jnp.concatenate([jnp.ones((1,), jnp.int32), (be[1:] != be[:-1]).astype(jnp.int32)])
    nu = n_used.reshape(1).astype(jnp.int32)
    tok = jnp.repeat(jnp.arange(n_tok, dtype=jnp.int32), TOP_K)
    tok_of_row = jnp.zeros((n_blocks * tme,), jnp.int32).at[dest.reshape(-1)].set(tok)

    xbuf = _dispatch(tok_of_row, h2, tme)
    act = _expert_up(be, first, nu, xbuf, w_e_up, b_e_up.reshape(1, N_EXPERTS, 1, -1), tme, 1024)
    ybuf = _expert_down(be, first, nu, act, w_e_down, b_e_down.reshape(1, N_EXPERTS, 1, -1), tme, 1024)
    out = _combine(dest, ybuf, x1, gate, 64)

    y_prompt = out[:n_prompt].reshape(n_batch, seq, d)
    y_sample = out[n_prompt:].reshape(n_db, n_new, d)
    k_meta = jnp.broadcast_to(qk[row_m:row_m + N_META, W_A:][None], (n_batch, N_META, W_A))
    v_meta = jnp.broadcast_to(z2[row_m:row_m + N_META, :W_A][None], (n_batch, N_META, W_A))
    k_prompt = jnp.concatenate([k_meta, qk[:n_prompt, W_A:].reshape(n_batch, seq, W_A)], axis=1)
    v_prompt = jnp.concatenate([v_meta, z2[:n_prompt, :W_A].reshape(n_batch, seq, W_A)], axis=1)
    k_prompt = k_prompt.reshape(1, n_batch, seq + N_META, H_A, VH_A)
    v_prompt = v_prompt.reshape(1, n_batch, seq + N_META, H_A, VH_A)
    k_sample = qk_s[:, W_A:].reshape(1, n_db, n_new, H_A, VH_A)
    v_sample = z2_s[:, :W_A].reshape(1, n_db, n_new, H_A, VH_A)
    return (y_prompt, y_sample, k_prompt, v_prompt, s_prompt[None], k_sample, v_sample, s_sample[None])
```

```python
import functools
import math

import jax
import jax.numpy as jnp
from jax import lax
from jax.experimental import pallas as pl
from jax.experimental.pallas import tpu as pltpu

F32 = jnp.float32
BF16 = jnp.bfloat16
HI = lax.Precision.HIGHEST

N_META = 16
H_A = 8
DH_A = 64
VH_A = 2 * DH_A
W_A = H_A * VH_A
H_B = 4
DK_B = 128
DV_B = 256
W_KB = H_B * DK_B
W_B = H_B * DV_B
GLA_RANK = 16
GLA_TAU = 16.0
GLA_CHUNK = 64
GLA_SUB = 16
N_EXPERTS = 32
TOP_K = 4
SWIGLU_LIMIT = 7.0
SWIGLU_ALPHA = 1.702
PAGE_SIZE = 128
EPS = 1e-6
LAM_INIT = 0.8 - 0.6 * math.exp(-0.3 * 0)

COL_QK = 0
COL_Z2 = 2 * W_A
Z2_COLS = W_A + 2 * W_KB + 2 * W_B
COL_A = COL_Z2 + Z2_COLS
COL_G = COL_A + GLA_RANK

LANES = 128
VMEM_LIMIT = 56 * 1024 * 1024

NEG = -1e30


def _mm(a, b, hi=False):
    if hi:
        return jnp.dot(a, b, precision=HI, preferred_element_type=F32)
    return jnp.dot(a.astype(BF16), b.astype(BF16), preferred_element_type=F32)


def _mm_nt(a, b, hi=False):
    dn = (((1,), (1,)), ((), ()))
    if hi:
        return lax.dot_general(a, b, dn, precision=HI, preferred_element_type=F32)
    return lax.dot_general(a.astype(BF16), b.astype(BF16), dn, preferred_element_type=F32)


def _mm_tn(a, b, hi=False):
    dn = (((0,), (0,)), ((), ()))
    if hi:
        return lax.dot_general(a, b, dn, precision=HI, preferred_element_type=F32)
    return lax.dot_general(a.astype(BF16), b.astype(BF16), dn, preferred_element_type=F32)


def _params(sem, vmem=VMEM_LIMIT):
    return pltpu.CompilerParams(dimension_semantics=sem, vmem_limit_bytes=vmem)


def _rms_kernel(x_ref, g_ref, o_ref):
    x = x_ref[...]
    ms = jnp.mean(x * x, axis=-1, keepdims=True)
    o_ref[...] = (x * lax.rsqrt(ms + EPS) * g_ref[...]).astype(o_ref.dtype)


def _rms(x, g, out_dtype, tm):
    m, d = x.shape
    return pl.pallas_call(
        _rms_kernel,
        grid=(m // tm,),
        in_specs=[pl.BlockSpec((tm, d), lambda i: (i, 0)),
                  pl.BlockSpec((1, d), lambda i: (0, 0))],
        out_specs=pl.BlockSpec((tm, d), lambda i: (i, 0)),
        out_shape=jax.ShapeDtypeStruct((m, d), out_dtype),
        compiler_params=_params(("arbitrary",)),
        name="rms_rows",
    )(x, g.reshape(1, d))


def _proj_kernel(*refs, mode, hi, tn):
    h_ref, w_ref = refs[0], refs[1]
    if mode == "qk":
        g_ref, o_ref = refs[2], refs[3]
        scr = refs[4:]
    elif mode == "loga":
        wfg_ref, bfg_ref, o_ref = refs[2], refs[3], refs[4]
        scr = refs[5:]
    else:
        o_ref = refs[2]
        scr = refs[3:]

    if hi:
        z = jnp.dot(h_ref[...], w_ref[...], precision=HI, preferred_element_type=F32)
    else:
        wbf = scr[0]

        @pl.when(pl.program_id(1) == 0)
        def _cast():
            wbf[...] = w_ref[...].astype(BF16)

        z = jnp.dot(h_ref[...], wbf[...], preferred_element_type=F32)

    if mode == "plain":
        o_ref[...] = z
    elif mode == "gate":
        o_ref[...] = jax.nn.sigmoid(z)
    elif mode == "qk":
        lane = lax.broadcasted_iota(jnp.int32, (z.shape[0], LANES), 1)
        lo = lane < DH_A
        g = g_ref[0]
        for c in range(tn // LANES):
            x = z[:, c * LANES:(c + 1) * LANES]
            x2 = x * x
            s0 = jnp.sum(jnp.where(lo, x2, 0.0), axis=-1, keepdims=True)
            s1 = jnp.sum(jnp.where(lo, 0.0, x2), axis=-1, keepdims=True)
            inv = jnp.where(lo, lax.rsqrt(s0 * (1.0 / DH_A) + EPS),
                            lax.rsqrt(s1 * (1.0 / DH_A) + EPS))
            o_ref[:, c * LANES:(c + 1) * LANES] = x * inv * g
    else:
        lane = lax.broadcasted_iota(jnp.int32, z.shape, 1)
        a = jnp.where(lane < GLA_RANK, z, 0.0)
        u = _mm(a, wfg_ref[...], hi) + bfg_ref[...]
        o_ref[...] = -(jnp.maximum(-u, 0.0) + jnp.log1p(jnp.exp(-jnp.abs(u)))) * (1.0 / GLA_TAU)


def _proj(h, w, col0, n_cols, tn, tm, mode, hi, extras=(), name="proj"):
    m, k = h.shape
    assert col0 % tn == 0 and n_cols % tn == 0 and m % tm == 0
    cb0 = col0 // tn
    if w.ndim == 3:
        w_spec = pl.BlockSpec((None, k, tn), lambda j, i: (0, 0, cb0 + j))
    else:
        w_spec = pl.BlockSpec((k, tn), lambda j, i: (0, cb0 + j))
    in_specs = [pl.BlockSpec((tm, k), lambda j, i: (i, 0)), w_spec]
    out_cols, out_tn = n_cols, tn
    if mode == "qk":
        in_specs.append(pl.BlockSpec((1, 1, LANES), lambda j, i: ((j * tn) // W_A, 0, 0)))
    elif mode == "loga":
        in_specs += [pl.BlockSpec((LANES, W_KB), lambda j, i: (0, 0)),
                     pl.BlockSpec((1, W_KB), lambda j, i: (0, 0))]
        out_cols, out_tn = W_KB, W_KB
    return pl.pallas_call(
        functools.partial(_proj_kernel, mode=mode, hi=hi, tn=tn),
        grid=(n_cols // tn, m // tm),
        in_specs=in_specs,
        out_specs=pl.BlockSpec((tm, out_tn), lambda j, i: (i, j)),
        out_shape=jax.ShapeDtypeStruct((m, out_cols), F32),
        scratch_shapes=[] if hi else [pltpu.VMEM((k, tn), BF16)],
        compiler_params=_params(("arbitrary", "arbitrary")),
        name=name,
    )(h, w, *extras)


def _attn_kernel(sc_ref, q_ref, k_ref, v_ref, km_ref, vm_ref, gs_ref, o_ref, kbf, vbf, m_s, l_s, acc_s, *,
                 tq, tk, n_heads):
    hg = pl.program_id(1)
    i = pl.program_id(2)
    half = tk // 2
    assert half == tq

    @pl.when(i == 0)
    def _cast():
        kbf[...] = k_ref[...].astype(BF16)
        vbf[...] = v_ref[...].astype(BF16)

    lam = sc_ref[0]
    q0 = i * tq
    width = 2 * tq
    lane = lax.broadcasted_iota(jnp.int32, (tq, LANES), 1)
    lo = lane < DH_A

    def tile_rel(n_keys):
        key = lax.broadcasted_iota(jnp.int32, (n_keys, width), 0)
        qry = lax.broadcasted_iota(jnp.int32, (n_keys, width), 1) % tq
        return key, qry, (qry - key).astype(F32)

    _, _, rel_full = tile_rel(tk)
    key, qry, rel_half = tile_rel(tq)

    def head_cols(g):
        return slice(g * LANES, (g + 1) * LANES)

    slopes, q12s = [], []
    for g in range(n_heads):
        slopes.append(sc_ref[1 + hg * n_heads + g])
        q = q_ref[:, head_cols(g)] * (DH_A ** -0.5)
        q12s.append(jnp.concatenate([jnp.where(lo, q, 0.0), jnp.where(lo, 0.0, q)], axis=0).astype(BF16))

    def scores(g, kc, n_keys, k0):
        s = lax.dot_general(kc, q12s[g], (((1,), (1,)), ((), ())), preferred_element_type=F32)
        rel = rel_full if n_keys == tk else rel_half
        return s + (rel + (q0 - k0).astype(F32)) * (-slopes[g])

    def update(g, s, vc):
        m = m_s[g]
        m_new = jnp.maximum(m, jnp.max(s, axis=0, keepdims=True))
        alpha = jnp.exp(m - m_new)
        p = jnp.exp(s - m_new)
        l_s[g] = alpha * l_s[g] + jnp.sum(p, axis=0, keepdims=True)
        acc_s[g] = alpha * acc_s[g] + _mm_tn(vc, p)
        m_s[g] = m_new

    for g in range(n_heads):
        s = scores(g, km_ref[:, head_cols(g)].astype(BF16), tq, -N_META)
        s = jnp.where(key < N_META, s, NEG)
        m0 = jnp.max(s, axis=0, keepdims=True)
        p = jnp.exp(s - m0)
        m_s[g] = m0
        l_s[g] = jnp.sum(p, axis=0, keepdims=True)
        acc_s[g] = _mm_tn(vm_ref[:, head_cols(g)], p)

    def full_chunk(j, carry):
        k0 = pl.multiple_of(j * tk, tk)
        for g in range(n_heads):
            update(g, scores(g, kbf[pl.ds(k0, tk), head_cols(g)], tk, k0), vbf[pl.ds(k0, tk), head_cols(g)])
        return carry

    def half_chunk(j, carry):
        k0 = pl.multiple_of(q0 - half, half)
        for g in range(n_heads):
            update(g, scores(g, kbf[pl.ds(k0, half), head_cols(g)], half, k0),
                   vbf[pl.ds(k0, half), head_cols(g)])
        return carry

    lax.fori_loop(0, q0 // tk, full_chunk, 0)
    lax.fori_loop(0, i % 2, half_chunk, 0)
    kq = pl.multiple_of(q0, tq)
    for g in range(n_heads):
        s = scores(g, kbf[pl.ds(kq, tq), head_cols(g)], tq, kq)
        s = jnp.where(qry >= key, s, NEG)
        update(g, s, vbf[pl.ds(kq, tq), head_cols(g)])
        accn = acc_s[g] / l_s[g]
        o = accn[:, :tq] - lam * accn[:, tq:]
        ms = jnp.mean(o * o, axis=0, keepdims=True)
        y = o * lax.rsqrt(ms + EPS) * gs_ref[...] * (1.0 - LAM_INIT)
        o_ref[:, head_cols(g)] = y.T.astype(o_ref.dtype)


def _prompt_attention(sc, qk, z2, g_sub, n_batch, seq, meta_row0, tq=128, tk=256, n_heads=8):
    nq = seq // tq
    n_prompt = n_batch * seq
    assert meta_row0 % tq == 0 and qk.shape[0] >= meta_row0 + tq and H_A % n_heads == 0
    meta_blk = meta_row0 // tq
    gw = n_heads * LANES
    kcol0 = W_A // gw
    return pl.pallas_call(
        functools.partial(_attn_kernel, tq=tq, tk=tk, n_heads=n_heads),
        grid=(n_batch, H_A // n_heads, nq),
        in_specs=[
            pl.BlockSpec(memory_space=pltpu.SMEM),
            pl.BlockSpec((tq, gw), lambda b, h, i: (b * nq + i, h)),
            pl.BlockSpec((seq, gw), lambda b, h, i: (b, kcol0 + h)),
            pl.BlockSpec((seq, gw), lambda b, h, i: (b, h)),
            pl.BlockSpec((tq, gw), lambda b, h, i: (meta_blk, kcol0 + h)),
            pl.BlockSpec((tq, gw), lambda b, h, i: (meta_blk, h)),
            pl.BlockSpec((VH_A, 1), lambda b, h, i: (0, 0)),
        ],
        out_specs=pl.BlockSpec((tq, gw), lambda b, h, i: (b * nq + i, h)),
        out_shape=jax.ShapeDtypeStruct((n_prompt, W_A), BF16),
        scratch_shapes=[pltpu.VMEM((seq, gw), BF16), pltpu.VMEM((seq, gw), BF16),
                        pltpu.VMEM((n_heads, 1, 2 * tq), F32), pltpu.VMEM((n_heads, 1, 2 * tq), F32),
                        pltpu.VMEM((n_heads, VH_A, 2 * tq), F32)],
        compiler_params=_params(("arbitrary", "arbitrary", "arbitrary")),
        name="prompt_attention",
    )(sc, qk, qk, z2, qk, z2, g_sub.reshape(VH_A, 1))


def _dec_kernel(pt_ref, sc_ref, q_ref, kn_ref, vn_ref, slc_ref, tcol_ref, hcol_ref, map0_ref, gs_ref, *rest,
                g_pages, n_steps, past, n_new):
    kp_refs = rest[:g_pages]
    vp_refs = rest[g_pages:2 * g_pages]
    o_ref = rest[2 * g_pages]
    m_s, l_s, acc_s = rest[2 * g_pages + 1:]
    j = pl.program_id(1)
    rows = PAGE_SIZE * H_A

    @pl.when(j == 0)
    def _init():
        m_s[...] = jnp.full(m_s.shape, NEG, F32)
        l_s[...] = jnp.zeros(l_s.shape, F32)
        acc_s[...] = jnp.zeros(acc_s.shape, F32)

    qcat = q_ref[...]
    q_hm, q_hi, q_lo = qcat[:, :2 * LANES], qcat[:, :LANES], qcat[:, 2 * LANES:]
    slc = slc_ref[...]
    tcol = tcol_ref[...]
    hcol = hcol_ref[...]

    def row_info(n_rows):
        r = lax.broadcasted_iota(jnp.int32, (n_rows, LANES), 0)
        tok = (r // H_A).astype(F32)
        own_head = (r % H_A).astype(F32) == hcol
        return tok, own_head

    def split(x):
        hi = x.astype(BF16)
        r = x - hi.astype(F32)
        mid = r.astype(BF16)
        return hi, mid, (r - mid.astype(F32)).astype(BF16)

    def halves(x):
        return x[:, :LANES], x[:, LANES:]

    def update(s, pv_fn):
        m_old = m_s[...]
        m_new = jnp.maximum(m_old, jnp.max(s, axis=0, keepdims=True))
        alpha = jnp.exp(m_old - m_new)
        p = jnp.exp(s - m_new)
        l_s[...] = alpha * l_s[...] + jnp.sum(p, axis=0, keepdims=True)
        acc_s[...] = acc_s[...] * alpha + pv_fn(p)
        m_s[...] = m_new

    tok, own_head = row_info(rows)
    rel = tcol - tok

    for u in range(g_pages):
        k_hi, k_mid, k_lo = split(kp_refs[u][...].reshape(rows, VH_A))
        hh, hm = halves(jnp.dot(k_hi, q_hm, preferred_element_type=F32))
        mh, mm = halves(jnp.dot(k_mid, q_hm, preferred_element_type=F32))
        small = (mm + jnp.dot(k_hi, q_lo, preferred_element_type=F32)
                 + jnp.dot(k_lo, q_hi, preferred_element_type=F32))
        page0 = (past - (j * g_pages + u) * PAGE_SIZE).astype(F32)
        s = hh + (hm + mh + small) + (rel + page0) * (-slc)
        s = jnp.where(own_head, s, NEG)

        def pv_page(p, u=u):
            p_hi, p_mid, p_lo = split(p)
            p_hm = jnp.concatenate([p_hi, p_mid], axis=1)
            v_hi, v_mid, v_lo = split(vp_refs[u][...].reshape(rows, VH_A))
            hh, hm = halves(_mm_tn(v_hi, p_hm))
            mh, mm = halves(_mm_tn(v_mid, p_hm))
            small = mm + _mm_tn(v_hi, p_lo) + _mm_tn(v_lo, p_hi)
            return hh + (hm + mh + small)

        update(s, pv_page)

    @pl.when(j == n_steps - 1)
    def _finish():
        qall = q_hi.astype(F32) + qcat[:, LANES:2 * LANES].astype(F32) + q_lo.astype(F32)
        tok_n, own_n = row_info(n_new * H_A)
        s = jnp.dot(kn_ref[...], qall, precision=HI, preferred_element_type=F32)
        s = s + (tcol - tok_n) * (-slc)
        s = jnp.where(jnp.logical_and(own_n, tok_n <= tcol), s, NEG)
        update(s, lambda p: _mm_tn(vn_ref[...], p, True))

        lam = sc_ref[0]
        on = acc_s[...] / l_s[...]
        d = on - lam * pltpu.roll(on, LANES - n_new, axis=1)
        d = jnp.where(map0_ref[...] > 0.5, d, 0.0)
        ms = jnp.mean(d * d, axis=0, keepdims=True)
        o_ref[...] = d * lax.rsqrt(ms + EPS) * gs_ref[...] * (1.0 - LAM_INIT)


def _decode_attention(page_table, sc, qk_s, z2_s, cache_k, cache_v, slopes, g_sub, g_pages=8):
    n_db, n_pages = page_table.shape
    n_sample = qk_s.shape[0]
    n_new = n_sample // n_db
    assert 2 * H_A * n_new == LANES
    past = n_pages * PAGE_SIZE
    n_steps = n_pages // g_pages

    q_s = (qk_s[:, :W_A] * (DH_A ** -0.5)).reshape(n_db, n_new, H_A, 2, DH_A)
    eye_m = jnp.eye(2, dtype=F32)
    qall = jnp.einsum("bthmd,mn->bndhmt", q_s, eye_m).reshape(n_db, VH_A, LANES)
    q_hi = qall.astype(BF16)
    q_res = qall - q_hi.astype(F32)
    q_mid = q_res.astype(BF16)
    qcat = jnp.concatenate([q_hi, q_mid, (q_res - q_mid.astype(F32)).astype(BF16)], axis=-1)
    col = jnp.arange(LANES)
    hcol = (col // (2 * n_new)).astype(F32).reshape(1, LANES)
    slc = slopes[col // (2 * n_new)].reshape(1, LANES)
    tcol = (col % n_new).astype(F32).reshape(1, LANES)
    map0 = ((col // n_new) % 2 == 0).astype(F32).reshape(1, LANES)
    kn = qk_s[:, W_A:].reshape(n_db, n_new * H_A, VH_A)
    vn = z2_s[:, :W_A].reshape(n_db, n_new * H_A, VH_A)

    def page_spec(u):
        return pl.BlockSpec((None, None, PAGE_SIZE, H_A, VH_A),
                            lambda d, j, pt: (0, pt[d * n_pages + j * g_pages + u], 0, 0, 0))

    row = lambda d, j, pt: (0, 0)
    in_specs = [
        pl.BlockSpec(memory_space=pltpu.SMEM),
        pl.BlockSpec((None, VH_A, 3 * LANES), lambda d, j, pt: (d, 0, 0)),
        pl.BlockSpec((None, n_new * H_A, VH_A), lambda d, j, pt: (d, 0, 0)),
        pl.BlockSpec((None, n_new * H_A, VH_A), lambda d, j, pt: (d, 0, 0)),
        pl.BlockSpec((1, LANES), row),
        pl.BlockSpec((1, LANES), row),
        pl.BlockSpec((1, LANES), row),
        pl.BlockSpec((1, LANES), row),
        pl.BlockSpec((VH_A, 1), row),
    ] + [page_spec(u) for u in range(g_pages)] * 2
    grid_spec = pltpu.PrefetchScalarGridSpec(
        num_scalar_prefetch=1,
        grid=(n_db, n_steps),
        in_specs=in_specs,
        out_specs=pl.BlockSpec((None, VH_A, LANES), lambda d, j, pt: (d, 0, 0)),
        scratch_shapes=[pltpu.VMEM((1, LANES), F32), pltpu.VMEM((1, LANES), F32),
                        pltpu.VMEM((VH_A, LANES), F32)],
    )
    out = pl.pallas_call(
        functools.partial(_dec_kernel, g_pages=g_pages, n_steps=n_steps, past=past, n_new=n_new),
        grid_spec=grid_spec,
        out_shape=jax.ShapeDtypeStruct((n_db, VH_A, LANES), F32),
        compiler_params=_params(("arbitrary", "arbitrary")),
        name="decode_attention",
    )(page_table.reshape(-1), sc, qcat, kn, vn, slc, tcol, hcol, map0, g_sub.reshape(VH_A, 1),
      *([cache_k] * g_pages), *([cache_v] * g_pages))
    out = out.reshape(n_db, VH_A, H_A, 2, n_new)[:, :, :, 0, :]
    return jnp.transpose(out, (0, 3, 2, 1)).reshape(n_sample, W_A)


def _gla_kernel(q_ref, k_ref, v_ref, la_ref, r_ref, s0_ref, gg_ref, o_ref, so_ref, state, *,
                chunk, sub, n_chunks, hi):
    state[...] = s0_ref[...]
    row = lax.broadcasted_iota(jnp.int32, (chunk, chunk), 0)
    col = lax.broadcasted_iota(jnp.int32, (chunk, chunk), 1)
    tri = (col <= row).astype(F32)
    tri_sub = (col <= (row // sub) * sub + (sub - 1)).astype(F32)
    causal = col <= row
    colblk = col // sub
    rowc = lax.broadcasted_iota(jnp.int32, (chunk, DK_B), 0)
    ones = jnp.ones((chunk, DK_B), F32)
    n_sub = chunk // sub

    def body(c, carry):
        c0 = pl.multiple_of(c * chunk, chunk)
        sl = pl.ds(c0, chunk)
        la = la_ref[sl, :]
        b = jnp.dot(tri, la, precision=HI, preferred_element_type=F32)
        rsub = jnp.dot(tri_sub, la, precision=HI, preferred_element_type=F32)
        b_end = b[chunk - 1:chunk, :]
        qs = q_ref[sl, :] * (DK_B ** -0.5)
        kk = k_ref[sl, :]
        vv = v_ref[sl, :]
        s_old = state[...]

        inter = _mm(qs * jnp.exp(b), s_old, hi)

        kt = kk * jnp.exp(rsub - b)
        qj = []
        for jb in range(n_sub):
            ref = rsub[jb * sub:jb * sub + 1, :]
            qj.append(qs * jnp.exp(jnp.where(rowc >= jb * sub, b - ref, -jnp.inf)))
        a4 = _mm_nt(jnp.concatenate(qj, axis=0), kt, hi)
        att = a4[:chunk]
        for jb in range(1, n_sub):
            att = jnp.where(colblk == jb, a4[jb * chunk:(jb + 1) * chunk], att)
        att = jnp.where(causal, att, 0.0)
        o = inter + _mm(att, vv, hi)

        dec = jnp.exp(_mm_tn(la, ones, True))
        kd = kk * jnp.exp(b_end - b)
        state[...] = jnp.concatenate([dec] * (DV_B // DK_B), axis=1) * s_old + _mm_tn(kd, vv, hi)

        ms = jnp.mean(o * o, axis=-1, keepdims=True)
        r = r_ref[sl, :]
        y = o * lax.rsqrt(ms + EPS) * gg_ref[...] * (r * jax.nn.sigmoid(r))
        o_ref[sl, :] = y.astype(o_ref.dtype)
        return carry

    lax.fori_loop(0, n_chunks, body, 0)
    so_ref[...] = state[...]


def _gla(z2, la, s0, g_gla, n_batch, seq, row0, chunk, sub, hi, out_dtype, name):
    rb0 = row0 // seq
    qc0 = W_A // DK_B
    kc0 = (W_A + W_KB) // DK_B
    vc0 = (W_A + 2 * W_KB) // DV_B
    rc0 = (W_A + 2 * W_KB + W_B) // DV_B
    per_batch_state = s0.shape[0] != 1
    return pl.pallas_call(
        functools.partial(_gla_kernel, chunk=chunk, sub=sub, n_chunks=seq // chunk, hi=hi),
        grid=(n_batch, H_B),
        in_specs=[
            pl.BlockSpec((seq, DK_B), lambda b, h: (rb0 + b, qc0 + h)),
            pl.BlockSpec((seq, DK_B), lambda b, h: (rb0 + b, kc0 + h)),
            pl.BlockSpec((seq, DV_B), lambda b, h: (rb0 + b, vc0 + h)),
            pl.BlockSpec((seq, DK_B), lambda b, h: (rb0 + b, h)),
            pl.BlockSpec((seq, DV_B), lambda b, h: (rb0 + b, rc0 + h)),
            pl.BlockSpec((None, None, DK_B, DV_B),
                         (lambda b, h: (b, h, 0, 0)) if per_batch_state else (lambda b, h: (0, h, 0, 0))),
            pl.BlockSpec((1, DV_B), lambda b, h: (0, 0)),
        ],
        out_specs=[
            pl.BlockSpec((seq, DV_B), lambda b, h: (b, h)),
            pl.BlockSpec((None, None, DK_B, DV_B), lambda b, h: (b, h, 0, 0)),
        ],
        out_shape=[jax.ShapeDtypeStruct((n_batch * seq, W_B), out_dtype),
                   jax.ShapeDtypeStruct((n_batch, H_B, DK_B, DV_B), F32)],
        scratch_shapes=[pltpu.VMEM((DK_B, DV_B), F32)],
        compiler_params=_params(("arbitrary", "arbitrary")),
        name=name,
    )(z2, z2, z2, la, z2, s0, g_gla.reshape(1, DV_B))


def _merge_kernel(oa_ref, og_ref, h_ref, wa_ref, wb_ref, wga_ref, wgb_ref, o_ref):
    h = h_ref[...]
    pa = jnp.dot(oa_ref[...], wa_ref[...], preferred_element_type=F32)
    ga = jax.nn.sigmoid(jnp.dot(h, wga_ref[...], preferred_element_type=F32))
    acc = ga * pa
    pb = jnp.dot(og_ref[...], wb_ref[...], preferred_element_type=F32)
    gb = jax.nn.sigmoid(jnp.dot(h, wgb_ref[...], preferred_element_type=F32))
    o_ref[...] = (acc + gb * pb).astype(o_ref.dtype)


def _merge(oa, og, h, wa, wb, wg, tm, tn):
    m, k = h.shape
    d = wa.shape[1]
    gb0 = d // tn
    return pl.pallas_call(
        _merge_kernel,
        grid=(d // tn, m // tm),
        in_specs=[
            pl.BlockSpec((tm, W_A), lambda j, i: (i, 0)),
            pl.BlockSpec((tm, W_B), lambda j, i: (i, 0)),
            pl.BlockSpec((tm, k), lambda j, i: (i, 0)),
            pl.BlockSpec((W_A, tn), lambda j, i: (0, j)),
            pl.BlockSpec((W_B, tn), lambda j, i: (0, j)),
            pl.BlockSpec((k, tn), lambda j, i: (0, j)),
            pl.BlockSpec((k, tn), lambda j, i: (0, gb0 + j)),
        ],
        out_specs=pl.BlockSpec((tm, tn), lambda j, i: (i, j)),
        out_shape=jax.ShapeDtypeStruct((m, d), BF16),
        compiler_params=_params(("arbitrary", "arbitrary")),
        name="merge",
    )(oa, og, h, wa, wb, wg, wg)


def _merge_s_kernel(oa_ref, og_ref, wa_ref, wb_ref, ga_ref, gb_ref, o_ref):
    pa = _mm(oa_ref[...], wa_ref[...], True)
    pb = _mm(og_ref[...], wb_ref[...], True)
    o_ref[...] = ga_ref[...] * pa + gb_ref[...] * pb


def _merge_sample(oa, og, wa, wb, sg, tn):
    m = oa.shape[0]
    d = wa.shape[1]
    gb0 = d // tn
    return pl.pallas_call(
        _merge_s_kernel,
        grid=(d // tn,),
        in_specs=[
            pl.BlockSpec((m, W_A), lambda j: (0, 0)),
            pl.BlockSpec((m, W_B), lambda j: (0, 0)),
            pl.BlockSpec((W_A, tn), lambda j: (0, j)),
            pl.BlockSpec((W_B, tn), lambda j: (0, j)),
            pl.BlockSpec((m, tn), lambda j: (0, j)),
            pl.BlockSpec((m, tn), lambda j: (0, gb0 + j)),
        ],
        out_specs=pl.BlockSpec((m, tn), lambda j: (0, j)),
        out_shape=jax.ShapeDtypeStruct((m, d), F32),
        compiler_params=_params(("arbitrary",)),
        name="merge_sample",
    )(oa, og, wa, wb, sg, sg)


def _outproj_kernel(mg_ref, w_ref, x_ref, gf_ref, x1_ref, h2_ref, *, hi):
    x1 = x_ref[...] + _mm(mg_ref[...], w_ref[...], hi)
    x1_ref[...] = x1
    ms = jnp.mean(x1 * x1, axis=-1, keepdims=True)
    h2_ref[...] = (x1 * lax.rsqrt(ms + EPS) * gf_ref[...]).astype(h2_ref.dtype)


def _outproj(mg, w, x, g_ffn, tm, hi, name):
    m, d = x.shape
    return pl.pallas_call(
        functools.partial(_outproj_kernel, hi=hi),
        grid=(m // tm,),
        in_specs=[
            pl.BlockSpec((tm, d), lambda i: (i, 0)),
            pl.BlockSpec((d, d), lambda i: (0, 0)),
            pl.BlockSpec((tm, d), lambda i: (i, 0)),
            pl.BlockSpec((1, d), lambda i: (0, 0)),
        ],
        out_specs=[pl.BlockSpec((tm, d), lambda i: (i, 0)), pl.BlockSpec((tm, d), lambda i: (i, 0))],
        out_shape=[jax.ShapeDtypeStruct((m, d), F32), jax.ShapeDtypeStruct((m, d), F32)],
        compiler_params=_params(("arbitrary",)),
        name=name,
    )(mg, w, x, g_ffn.reshape(1, d))


def _route_kernel(h_ref, w_ref, b_ref, c0_ref, idx_ref, gate_ref, rank_ref, cnt_ref, carry, *, tm, hi):
    @pl.when(pl.program_id(0) == 0)
    def _init():
        carry[...] = c0_ref[...]

    logits = _mm(h_ref[...], w_ref[...], hi) + b_ref[...]
    lane = lax.broadcasted_iota(jnp.int32, (tm, N_EXPERTS), 1).astype(F32)
    work = logits
    vals, idxs, sels = [], [], []
    for _ in range(TOP_K):
        mx = jnp.max(work, axis=-1, keepdims=True)
        ix = jnp.min(jnp.where(work == mx, lane, float(N_EXPERTS)), axis=-1, keepdims=True)
        sel = lane == ix
        vals.append(mx)
        idxs.append(ix)
        sels.append(sel)
        work = jnp.where(sel, -jnp.inf, work)
    es = [jnp.exp(v - vals[0]) for v in vals]
    tot = es[0] + es[1] + es[2] + es[3]
    gate_ref[...] = jnp.concatenate([e / tot for e in es], axis=1)
    idx_ref[...] = jnp.concatenate(idxs, axis=1).astype(jnp.int32)

    onehot = jnp.zeros((tm, N_EXPERTS), F32)
    for sel in sels:
        onehot = onehot + sel.astype(F32)
    r = lax.broadcasted_iota(jnp.int32, (tm, tm), 0)
    c = lax.broadcasted_iota(jnp.int32, (tm, tm), 1)
    before = (c < r).astype(BF16)
    cum = jnp.dot(before, onehot.astype(BF16), preferred_element_type=F32) + carry[...]
    ranks = [jnp.sum(jnp.where(sel, cum, 0.0), axis=-1, keepdims=True) for sel in sels]
    rank_ref[...] = jnp.concatenate(ranks, axis=1).astype(jnp.int32)
    carry[...] = carry[...] + jnp.sum(onehot, axis=0, keepdims=True)
    cnt_ref[...] = carry[...]


def _route(h, n_tok, w_router, b_router, c0, tm, hi, name):
    d = h.shape[1]
    assert n_tok % tm == 0
    return pl.pallas_call(
        functools.partial(_route_kernel, tm=tm, hi=hi),
        grid=(n_tok // tm,),
        in_specs=[
            pl.BlockSpec((tm, d), lambda i: (i, 0)),
            pl.BlockSpec((d, N_EXPERTS), lambda i: (0, 0)),
            pl.BlockSpec((1, N_EXPERTS), lambda i: (0, 0)),
            pl.BlockSpec((1, N_EXPERTS), lambda i: (0, 0)),
        ],
        out_specs=[
            pl.BlockSpec((tm, TOP_K), lambda i: (i, 0)),
            pl.BlockSpec((tm, TOP_K), lambda i: (i, 0)),
            pl.BlockSpec((tm, TOP_K), lambda i: (i, 0)),
            pl.BlockSpec((1, N_EXPERTS), lambda i: (0, 0)),
        ],
        out_shape=[
            jax.ShapeDtypeStruct((n_tok, TOP_K), jnp.int32),
            jax.ShapeDtypeStruct((n_tok, TOP_K), F32),
            jax.ShapeDtypeStruct((n_tok, TOP_K), jnp.int32),
            jax.ShapeDtypeStruct((1, N_EXPERTS), F32),
        ],
        scratch_shapes=[pltpu.VMEM((1, N_EXPERTS), F32)],
        compiler_params=_params(("arbitrary",)),
        name=name,
    )(h, w_router, b_router.reshape(1, N_EXPERTS), c0)


def _up_kernel(be_ref, first_ref, nu_ref, x_ref, wg_ref, wl_ref, bg_ref, bl_ref, o_ref, wgb, wlb):
    b = pl.program_id(1)

    @pl.when(first_ref[b] == 1)
    def _cast():
        wgb[...] = wg_ref[...].astype(BF16)
        wlb[...] = wl_ref[...].astype(BF16)

    @pl.when(b < nu_ref[0])
    def _compute():
        x = x_ref[...]
        g = jnp.dot(x, wgb[...], preferred_element_type=F32) + bg_ref[...]
        lin = jnp.dot(x, wlb[...], preferred_element_type=F32) + bl_ref[...]
        g = jnp.minimum(g, SWIGLU_LIMIT)
        lin = jnp.clip(lin, -SWIGLU_LIMIT, SWIGLU_LIMIT)
        glu = g * jax.nn.sigmoid(SWIGLU_ALPHA * g)
        o_ref[...] = ((lin + 1.0) * glu).astype(o_ref.dtype)

    @pl.when(b >= nu_ref[0])
    def _unused():
        o_ref[...] = jnp.zeros(o_ref.shape, o_ref.dtype)


def _expert_up(be, first, nu, xbuf, w_up, b_up, tm, th):
    rows, d = xbuf.shape
    d_e = w_up.shape[-1] // 2
    nj = d_e // th
    grid_spec = pltpu.PrefetchScalarGridSpec(
        num_scalar_prefetch=3,
        grid=(nj, rows // tm),
        in_specs=[
            pl.BlockSpec((tm, d), lambda j, b, be, fi, nu: (b, 0)),
            pl.BlockSpec((None, None, d, th), lambda j, b, be, fi, nu: (0, be[b], 0, j)),
            pl.BlockSpec((None, None, d, th), lambda j, b, be, fi, nu: (0, be[b], 0, nj + j)),
            pl.BlockSpec((None, None, 1, th), lambda j, b, be, fi, nu: (0, be[b], 0, j)),
            pl.BlockSpec((None, None, 1, th), lambda j, b, be, fi, nu: (0, be[b], 0, nj + j)),
        ],
        out_specs=pl.BlockSpec((tm, th), lambda j, b, be, fi, nu: (b, j)),
        scratch_shapes=[pltpu.VMEM((d, th), BF16), pltpu.VMEM((d, th), BF16)],
    )
    return pl.pallas_call(
        _up_kernel,
        grid_spec=grid_spec,
        out_shape=jax.ShapeDtypeStruct((rows, d_e), BF16),
        compiler_params=_params(("arbitrary", "arbitrary")),
        name="expert_up",
    )(be, first, nu, xbuf, w_up, w_up, b_up, b_up)


def _down_kernel(be_ref, first_ref, nu_ref, a_ref, w_ref, b_ref, o_ref, wb):
    b = pl.program_id(1)

    @pl.when(first_ref[b] == 1)
    def _cast():
        wb[...] = w_ref[...].astype(BF16)

    @pl.when(b < nu_ref[0])
    def _compute():
        o_ref[...] = jnp.dot(a_ref[...], wb[...], preferred_element_type=F32) + b_ref[...]

    @pl.when(b >= nu_ref[0])
    def _unused():
        o_ref[...] = jnp.zeros(o_ref.shape, o_ref.dtype)


def _expert_down(be, first, nu, act, w_down, b_down, tm, tn):
    rows, d_e = act.shape
    d = w_down.shape[-1]
    grid_spec = pltpu.PrefetchScalarGridSpec(
        num_scalar_prefetch=3,
        grid=(d // tn, rows // tm),
        in_specs=[
            pl.BlockSpec((tm, d_e), lambda j, b, be, fi, nu: (b, 0)),
            pl.BlockSpec((None, None, d_e, tn), lambda j, b, be, fi, nu: (0, be[b], 0, j)),
            pl.BlockSpec((None, None, 1, tn), lambda j, b, be, fi, nu: (0, be[b], 0, j)),
        ],
        out_specs=pl.BlockSpec((tm, tn), lambda j, b, be, fi, nu: (b, j)),
        scratch_shapes=[pltpu.VMEM((d_e, tn), BF16)],
    )
    return pl.pallas_call(
        _down_kernel,
        grid_spec=grid_spec,
        out_shape=jax.ShapeDtypeStruct((rows, d), F32),
        compiler_params=_params(("arbitrary", "arbitrary")),
        name="expert_down",
    )(be, first, nu, act, w_down, b_down)


def _row_copy(src_hbm, row, dst, sem):
    return pltpu.make_async_copy(src_hbm.at[pl.ds(row, 1)], dst, sem)


def _dispatch_kernel(tok_ref, h_hbm, o_ref, buf, sem, *, tm):
    b = pl.program_id(0)

    def start_tile(blk, slot):
        def body(r, carry):
            _row_copy(h_hbm, tok_ref[blk * tm + r], buf.at[slot, pl.ds(r, 1)], sem.at[slot]).start()
            return carry
        lax.fori_loop(0, tm, body, 0, unroll=8)

    @pl.when(b == 0)
    def _first():
        start_tile(0, 0)

    @pl.when(b + 1 < pl.num_programs(0))
    def _next():
        start_tile(b + 1, (b + 1) % 2)

    slot = b % 2

    def wait_body(r, carry):
        _row_copy(h_hbm, 0, buf.at[slot, pl.ds(r, 1)], sem.at[slot]).wait()
        return carry
    lax.fori_loop(0, tm, wait_body, 0, unroll=8)
    o_ref[...] = buf[slot].astype(o_ref.dtype)


def _dispatch(tok_of_row, h, tm):
    rows = tok_of_row.shape[0]
    d = h.shape[1]
    grid_spec = pltpu.PrefetchScalarGridSpec(
        num_scalar_prefetch=1,
        grid=(rows // tm,),
        in_specs=[pl.BlockSpec(memory_space=pl.ANY)],
        out_specs=pl.BlockSpec((tm, d), lambda b, tok: (b, 0)),
        scratch_shapes=[pltpu.VMEM((2, tm, d), F32), pltpu.SemaphoreType.DMA((2,))],
    )
    return pl.pallas_call(
        functools.partial(_dispatch_kernel, tm=tm),
        grid_spec=grid_spec,
        out_shape=jax.ShapeDtypeStruct((rows, d), BF16),
        compiler_params=_params(("arbitrary",)),
        name="moe_dispatch",
    )(tok_of_row, h)


def _combine_kernel(dest_ref, y_hbm, x_ref, gate_ref, o_ref, buf, sem, *, tm):
    i = pl.program_id(0)

    def start_tile(blk, slot):
        def body(r, carry):
            for k in range(TOP_K):
                row = dest_ref[(blk * tm + r) * TOP_K + k]
                _row_copy(y_hbm, row, buf.at[slot, k, pl.ds(r, 1)], sem.at[slot]).start()
            return carry
        lax.fori_loop(0, tm, body, 0, unroll=4)

    @pl.when(i == 0)
    def _first():
        start_tile(0, 0)

    @pl.when(i + 1 < pl.num_programs(0))
    def _next():
        start_tile(i + 1, (i + 1) % 2)

    slot = i % 2

    def wait_body(r, carry):
        for k in range(TOP_K):
            _row_copy(y_hbm, 0, buf.at[slot, k, pl.ds(r, 1)], sem.at[slot]).wait()
        return carry
    lax.fori_loop(0, tm, wait_body, 0, unroll=4)
    g = gate_ref[...]
    y = g[:, 0:1] * buf[slot, 0]
    for k in range(1, TOP_K):
        y = y + g[:, k:k + 1] * buf[slot, k]
    o_ref[...] = x_ref[...] + y


def _combine(dest, ybuf, x1, gate, tm):
    n_tok, d = x1.shape
    assert n_tok % tm == 0
    grid_spec = pltpu.PrefetchScalarGridSpec(
        num_scalar_prefetch=1,
        grid=(n_tok // tm,),
        in_specs=[pl.BlockSpec(memory_space=pl.ANY),
                  pl.BlockSpec((tm, d), lambda i, dest: (i, 0)),
                  pl.BlockSpec((tm, TOP_K), lambda i, dest: (i, 0))],
        out_specs=pl.BlockSpec((tm, d), lambda i, dest: (i, 0)),
        scratch_shapes=[pltpu.VMEM((2, TOP_K, tm, d), F32), pltpu.SemaphoreType.DMA((2,))],
    )
    return pl.pallas_call(
        functools.partial(_combine_kernel, tm=tm),
        grid_spec=grid_spec,
        out_shape=jax.ShapeDtypeStruct((n_tok, d), F32),
        compiler_params=_params(("arbitrary",)),
        name="moe_combine",
    )(dest.reshape(-1), ybuf, x1, gate)


def kernel(x_prompt, x_sample, cache_k, cache_v, state_gla, page_table, meta_tokens, g_mix, w_in, g_q, g_k, lam_q1, lam_k1, lam_q2, lam_k2, g_sub, w_fg, b_fg, g_gla, w_br_a, w_br_b, w_out, g_ffn, w_router, b_router, w_e_up, b_e_up, w_e_down, b_e_down):
    assert w_in.shape[0] == 1, "single-layer trunk"
    n_batch, seq, d = x_prompt.shape
    n_db, n_new, _ = x_sample.shape
    assert 2 * H_A * n_new == LANES
    n_prompt = n_batch * seq
    n_sample = n_db * n_new
    n_tok = n_prompt + n_sample
    row_m = n_prompt

    def layer0(x):
        return x.reshape(x.shape[1:])

    w = layer0(w_in)
    w_a, w_b, w_o, w_r = layer0(w_br_a), layer0(w_br_b), layer0(w_out), layer0(w_router)
    lam =(jnp.exp(jnp.sum(lam_q1[0] * lam_k1[0])) - jnp.exp(jnp.sum(lam_q2[0] * lam_k2[0])) + LAM_INIT)
    slopes = jnp.exp2(-8.0 * jnp.arange(1, H_A + 1, dtype=F32) / H_A)
    sc = jnp.concatenate([lam.reshape(1), slopes]).astype(F32)
    g_qk = jnp.stack([g_q[0].reshape(1, VH_A), g_k[0].reshape(1, VH_A)])
    wfg_pad = jnp.zeros((LANES, W_KB), F32).at[:GLA_RANK].set(w_fg[0])
    bfg = b_fg[0].reshape(1, W_KB)
    w_gates = w[:, COL_G:]

    pad_rows = (-(n_prompt + N_META)) % (5 * 13 * LANES)
    xm = jnp.concatenate([x_prompt.reshape(n_prompt, d), meta_tokens.astype(F32),
                          jnp.zeros((pad_rows, d), F32)], axis=0)
    n_rows = xm.shape[0]
    tm_proj = n_rows // 5
    tm_row = n_rows // 13
    hm = _rms(xm, g_mix[0], BF16, tm=tm_row)
    qk = _proj(hm, w_in, COL_QK, 2 * W_A, 512, tm_proj, "qk", False, (g_qk,), name="proj_qk")
    z2 = _proj(hm, w_in, COL_Z2, Z2_COLS, 512, tm_proj, "plain", False, name="proj_z2")
    la = _proj(hm, w_in, COL_A, LANES, LANES, tm_proj, "loga", False, (wfg_pad, bfg), name="proj_loga")

    oa_p = _prompt_attention(sc, qk, z2, g_sub[0], n_batch, seq, row_m)
    zero_state = jnp.zeros((1, H_B, DK_B, DV_B), F32)
    _, s_meta = _gla(z2, la, zero_state, g_gla[0], 1, N_META, row_m, N_META, N_META, False, BF16, "gla_meta")
    og_p, s_prompt = _gla(z2, la, s_meta, g_gla[0], n_batch, seq, 0, GLA_CHUNK, GLA_SUB, False, BF16,
                          "gla_prompt")
    tail = jnp.zeros((n_rows - n_prompt, W_A), BF16)
    mg = _merge(jnp.concatenate([oa_p, tail], axis=0), jnp.concatenate([og_p, tail], axis=0), hm,
                w_a.astype(BF16), w_b.astype(BF16), w_gates.astype(BF16), tm_row, 512)
    x1_m, h2_m = _outproj(mg, w_o.astype(BF16), xm, g_ffn[0], tm_row // 2, False, "outproj")

    xs = x_sample.reshape(n_sample, d)
    hs = _rms(xs, g_mix[0], F32, tm=n_sample)
    qk_s = _proj(hs, w_in, COL_QK, 2 * W_A, 512, n_sample, "qk", True, (g_qk,), name="proj_qk_s")
    z2_s = _proj(hs, w_in, COL_Z2, Z2_COLS, 512, n_sample, "plain", True, name="proj_z2_s")
    la_s = _proj(hs, w_in, COL_A, LANES, LANES, n_sample, "loga", True, (wfg_pad, bfg), name="proj_loga_s")
    sg_s = _proj(hs, w_gates, 0, 2 * d, 512, n_sample, "gate", True, name="proj_gate_s")

    oa_s = _decode_attention(page_table, sc, qk_s, z2_s, cache_k, cache_v, slopes, g_sub[0])
    og_s, s_sample = _gla(z2_s, la_s, layer0(state_gla), g_gla[0], n_db, n_new, 0, n_new, n_new, True, F32,
                          "gla_sample")
    mg_s = _merge_sample(oa_s, og_s, w_a, w_b, sg_s, 512)
    x1_s, h2_s = _outproj(mg_s, w_o, xs, g_ffn[0], n_sample, True, "outproj_sample")

    c0 = jnp.zeros((1, N_EXPERTS), F32)
    idx_m, gate_m, rank_m, cnt_m = _route(h2_m, n_prompt, w_r, b_router[0], c0, 512, False, "route_prompt")
    idx_s, gate_s, rank_s, cnt = _route(h2_s, n_sample, w_r, b_router[0], cnt_m, n_sample, True,
                                        "route_sample")
    eidx = jnp.concatenate([idx_m, idx_s], axis=0)
    gate = jnp.concatenate([gate_m, gate_s], axis=0)
    rank = jnp.concatenate([rank_m, rank_s], axis=0)
    h2 = jnp.concatenate([h2_m[:n_prompt], h2_s], axis=0)
    x1 = jnp.concatenate([x1_m[:n_prompt], x1_s], axis=0)

    tme = 256
    n_blocks = -(-(n_tok * TOP_K) // tme) + N_EXPERTS
    counts = cnt.reshape(N_EXPERTS).astype(jnp.int32)
    padded = (counts + tme - 1) // tme * tme
    pend = jnp.cumsum(padded)
    pstart = pend - padded
    dest = pstart[eidx] + rank
    n_used = pend[-1] // tme
    blk = jnp.arange(n_blocks, dtype=jnp.int32)
    be = jnp.minimum(jnp.sum(pend[None, :] <= (blk * tme)[:, None], axis=1), N_EXPERTS - 1).astype(jnp.int32)
    be = jnp.where(blk < n_used, be, be[jnp.maximum(n_used - 1, 0)])
    first = jnp.concatenate([jnp.ones((1,), jnp.int32), (be[1:] != be[:-1]).astype(jnp.int32)])
    nu = n_used.reshape(1).astype(jnp.int32)
    tok = jnp.repeat(jnp.arange(n_tok, dtype=jnp.int32), TOP_K)
    tok_of_row = (jnp.arange(n_blocks * tme, dtype=jnp.int32) % n_tok).at[dest.reshape(-1)].set(tok)

    xbuf = _dispatch(tok_of_row, h2, tme)
    act = _expert_up(be, first, nu, xbuf, w_e_up, b_e_up.reshape(1, N_EXPERTS, 1, -1), tme, 1024)
    ybuf = _expert_down(be, first, nu, act, w_e_down, b_e_down.reshape(1, N_EXPERTS, 1, -1), tme, 1024)
    out = _combine(dest, ybuf, x1, gate, 64)

    y_prompt = out[:n_prompt].reshape(n_batch, seq, d)
    y_sample = out[n_prompt:].reshape(n_db, n_new, d)
    k_meta = jnp.broadcast_to(qk[row_m:row_m + N_META, W_A:][None], (n_batch, N_META, W_A))
    v_meta = jnp.broadcast_to(z2[row_m:row_m + N_META, :W_A][None], (n_batch, N_META, W_A))
    k_prompt = jnp.concatenate([k_meta, qk[:n_prompt, W_A:].reshape(n_batch, seq, W_A)], axis=1)
    v_prompt = jnp.concatenate([v_meta, z2[:n_prompt, :W_A].reshape(n_batch, seq, W_A)], axis=1)
    k_prompt = k_prompt.reshape(1, n_batch, seq + N_META, H_A, VH_A)
    v_prompt = v_prompt.reshape(1, n_batch, seq + N_META, H_A, VH_A)
    k_sample = qk_s[:, W_A:].reshape(1, n_db, n_new, H_A, VH_A)
    v_sample = z2_s[:, :W_A].reshape(1, n_db, n_new, H_A, VH_A)
    return (y_prompt, y_sample, k_prompt, v_prompt, s_prompt[None], k_sample, v_sample, s_sample[None])
```

```python
import functools
import math

import jax
import jax.numpy as jnp
from jax import lax
from jax.experimental import pallas as pl
from jax.experimental.pallas import tpu as pltpu

F32 = jnp.float32
BF16 = jnp.bfloat16
HI = lax.Precision.HIGHEST

N_META = 16
H_A = 8
DH_A = 64
VH_A = 2 * DH_A
W_A = H_A * VH_A
H_B = 4
DK_B = 128
DV_B = 256
W_KB = H_B * DK_B
W_B = H_B * DV_B
GLA_RANK = 16
GLA_TAU = 16.0
GLA_CHUNK = 64
GLA_SUB = 16
N_EXPERTS = 32
TOP_K = 4
SWIGLU_LIMIT = 7.0
SWIGLU_ALPHA = 1.702
PAGE_SIZE = 128
EPS = 1e-6
LAM_INIT = 0.8 - 0.6 * math.exp(-0.3 * 0)

COL_QK = 0
COL_Z2 = 2 * W_A
Z2_COLS = W_A + 2 * W_KB + 2 * W_B
COL_A = COL_Z2 + Z2_COLS
COL_G = COL_A + GLA_RANK

LANES = 128
VMEM_LIMIT = 56 * 1024 * 1024

NEG = -1e30


def _mm(a, b, hi=False):
    if hi:
        return jnp.dot(a, b, precision=HI, preferred_element_type=F32)
    return jnp.dot(a.astype(BF16), b.astype(BF16), preferred_element_type=F32)


def _mm_nt(a, b, hi=False):
    dn = (((1,), (1,)), ((), ()))
    if hi:
        return lax.dot_general(a, b, dn, precision=HI, preferred_element_type=F32)
    return lax.dot_general(a.astype(BF16), b.astype(BF16), dn, preferred_element_type=F32)


def _mm_tn(a, b, hi=False):
    dn = (((0,), (0,)), ((), ()))
    if hi:
        return lax.dot_general(a, b, dn, precision=HI, preferred_element_type=F32)
    return lax.dot_general(a.astype(BF16), b.astype(BF16), dn, preferred_element_type=F32)


def _params(sem, vmem=VMEM_LIMIT):
    return pltpu.CompilerParams(dimension_semantics=sem, vmem_limit_bytes=vmem)


def _rms_kernel(x_ref, g_ref, o_ref):
    x = x_ref[...]
    ms = jnp.mean(x * x, axis=-1, keepdims=True)
    o_ref[...] = (x * lax.rsqrt(ms + EPS) * g_ref[...]).astype(o_ref.dtype)


def _rms(x, g, out_dtype, tm):
    m, d = x.shape
    return pl.pallas_call(
        _rms_kernel,
        grid=(m // tm,),
        in_specs=[pl.BlockSpec((tm, d), lambda i: (i, 0)),
                  pl.BlockSpec((1, d), lambda i: (0, 0))],
        out_specs=pl.BlockSpec((tm, d), lambda i: (i, 0)),
        out_shape=jax.ShapeDtypeStruct((m, d), out_dtype),
        compiler_params=_params(("arbitrary",)),
        name="rms_rows",
    )(x, g.reshape(1, d))


def _proj_kernel(*refs, mode, hi, tn):
    h_ref, w_ref = refs[0], refs[1]
    if mode == "qk":
        g_ref, o_ref = refs[2], refs[3]
        scr = refs[4:]
    elif mode == "loga":
        wfg_ref, bfg_ref, o_ref = refs[2], refs[3], refs[4]
        scr = refs[5:]
    else:
        o_ref = refs[2]
        scr = refs[3:]

    if hi:
        z = jnp.dot(h_ref[...], w_ref[...], precision=HI, preferred_element_type=F32)
    else:
        wbf = scr[0]

        @pl.when(pl.program_id(1) == 0)
        def _cast():
            wbf[...] = w_ref[...].astype(BF16)

        z = jnp.dot(h_ref[...], wbf[...], preferred_element_type=F32)

    if mode == "plain":
        o_ref[...] = z
    elif mode == "gate":
        o_ref[...] = jax.nn.sigmoid(z)
    elif mode == "qk":
        lane = lax.broadcasted_iota(jnp.int32, (z.shape[0], LANES), 1)
        lo = lane < DH_A
        g = g_ref[0]
        for c in range(tn // LANES):
            x = z[:, c * LANES:(c + 1) * LANES]
            x2 = x * x
            s0 = jnp.sum(jnp.where(lo, x2, 0.0), axis=-1, keepdims=True)
            s1 = jnp.sum(jnp.where(lo, 0.0, x2), axis=-1, keepdims=True)
            inv = jnp.where(lo, lax.rsqrt(s0 * (1.0 / DH_A) + EPS),
                            lax.rsqrt(s1 * (1.0 / DH_A) + EPS))
            o_ref[:, c * LANES:(c + 1) * LANES] = x * inv * g
    else:
        lane = lax.broadcasted_iota(jnp.int32, z.shape, 1)
        a = jnp.where(lane < GLA_RANK, z, 0.0)
        u = _mm(a, wfg_ref[...], hi) + bfg_ref[...]
        o_ref[...] = -(jnp.maximum(-u, 0.0) + jnp.log1p(jnp.exp(-jnp.abs(u)))) * (1.0 / GLA_TAU)


def _proj(h, w, col0, n_cols, tn, tm, mode, hi, extras=(), name="proj"):
    m, k = h.shape
    assert col0 % tn == 0 and n_cols % tn == 0 and m % tm == 0
    cb0 = col0 // tn
    if w.ndim == 3:
        w_spec = pl.BlockSpec((None, k, tn), lambda j, i: (0, 0, cb0 + j))
    else:
        w_spec = pl.BlockSpec((k, tn), lambda j, i: (0, cb0 + j))
    in_specs = [pl.BlockSpec((tm, k), lambda j, i: (i, 0)), w_spec]
    out_cols, out_tn = n_cols, tn
    if mode == "qk":
        in_specs.append(pl.BlockSpec((1, 1, LANES), lambda j, i: ((j * tn) // W_A, 0, 0)))
    elif mode == "loga":
        in_specs += [pl.BlockSpec((LANES, W_KB), lambda j, i: (0, 0)),
                     pl.BlockSpec((1, W_KB), lambda j, i: (0, 0))]
        out_cols, out_tn = W_KB, W_KB
    return pl.pallas_call(
        functools.partial(_proj_kernel, mode=mode, hi=hi, tn=tn),
        grid=(n_cols // tn, m // tm),
        in_specs=in_specs,
        out_specs=pl.BlockSpec((tm, out_tn), lambda j, i: (i, j)),
        out_shape=jax.ShapeDtypeStruct((m, out_cols), F32),
        scratch_shapes=[] if hi else [pltpu.VMEM((k, tn), BF16)],
        compiler_params=_params(("arbitrary", "arbitrary")),
        name=name,
    )(h, w, *extras)


def _attn_kernel(sc_ref, q_ref, k_ref, v_ref, km_ref, vm_ref, gs_ref, o_ref, kbf, vbf, m_s, l_s, acc_s, *,
                 tq, tk, n_heads):
    hg = pl.program_id(1)
    i = pl.program_id(2)
    half = tk // 2
    assert half == tq

    @pl.when(i == 0)
    def _cast():
        kbf[...] = k_ref[...].astype(BF16)
        vbf[...] = v_ref[...].astype(BF16)

    lam = sc_ref[0]
    q0 = i * tq
    width = 2 * tq
    lane = lax.broadcasted_iota(jnp.int32, (tq, LANES), 1)
    lo = lane < DH_A

    def tile_rel(n_keys):
        key = lax.broadcasted_iota(jnp.int32, (n_keys, width), 0)
        qry = lax.broadcasted_iota(jnp.int32, (n_keys, width), 1) % tq
        return key, qry, (qry - key).astype(F32)

    _, _, rel_full = tile_rel(tk)
    key, qry, rel_half = tile_rel(tq)

    def head_cols(g):
        return slice(g * LANES, (g + 1) * LANES)

    slopes, q12s = [], []
    for g in range(n_heads):
        slopes.append(sc_ref[1 + hg * n_heads + g])
        q = q_ref[:, head_cols(g)] * (DH_A ** -0.5)
        q12s.append(jnp.concatenate([jnp.where(lo, q, 0.0), jnp.where(lo, 0.0, q)], axis=0).astype(BF16))

    def scores(g, kc, n_keys, k0):
        s = lax.dot_general(kc, q12s[g], (((1,), (1,)), ((), ())), preferred_element_type=F32)
        rel = rel_full if n_keys == tk else rel_half
        return s + (rel + (q0 - k0).astype(F32)) * (-slopes[g])

    def update(g, s, vc):
        m = m_s[g]
        m_new = jnp.maximum(m, jnp.max(s, axis=0, keepdims=True))
        alpha = jnp.exp(m - m_new)
        p = jnp.exp(s - m_new)
        l_s[g] = alpha * l_s[g] + jnp.sum(p, axis=0, keepdims=True)
        acc_s[g] = alpha * acc_s[g] + _mm_tn(vc, p)
        m_s[g] = m_new

    for g in range(n_heads):
        s = scores(g, km_ref[:, head_cols(g)].astype(BF16), tq, -N_META)
        s = jnp.where(key < N_META, s, NEG)
        m0 = jnp.max(s, axis=0, keepdims=True)
        p = jnp.exp(s - m0)
        m_s[g] = m0
        l_s[g] = jnp.sum(p, axis=0, keepdims=True)
        acc_s[g] = _mm_tn(vm_ref[:, head_cols(g)], p)

    def full_chunk(j, carry):
        k0 = pl.multiple_of(j * tk, tk)
        for g in range(n_heads):
            update(g, scores(g, kbf[pl.ds(k0, tk), head_cols(g)], tk, k0), vbf[pl.ds(k0, tk), head_cols(g)])
        return carry

    def half_chunk(j, carry):
        k0 = pl.multiple_of(q0 - half, half)
        for g in range(n_heads):
            update(g, scores(g, kbf[pl.ds(k0, half), head_cols(g)], half, k0),
                   vbf[pl.ds(k0, half), head_cols(g)])
        return carry

    lax.fori_loop(0, q0 // tk, full_chunk, 0)
    lax.fori_loop(0, i % 2, half_chunk, 0)
    kq = pl.multiple_of(q0, tq)
    for g in range(n_heads):
        s = scores(g, kbf[pl.ds(kq, tq), head_cols(g)], tq, kq)
        s = jnp.where(qry >= key, s, NEG)
        update(g, s, vbf[pl.ds(kq, tq), head_cols(g)])
        accn = acc_s[g] / l_s[g]
        o = accn[:, :tq] - lam * accn[:, tq:]
        ms = jnp.mean(o * o, axis=0, keepdims=True)
        y = o * lax.rsqrt(ms + EPS) * gs_ref[...] * (1.0 - LAM_INIT)
        o_ref[:, head_cols(g)] = y.T.astype(o_ref.dtype)


def _prompt_attention(sc, qk, z2, g_sub, n_batch, seq, meta_row0, tq=128, tk=256, n_heads=8):
    nq = seq // tq
    n_prompt = n_batch * seq
    assert meta_row0 % tq == 0 and qk.shape[0] >= meta_row0 + tq and H_A % n_heads == 0
    meta_blk = meta_row0 // tq
    gw = n_heads * LANES
    kcol0 = W_A // gw
    return pl.pallas_call(
        functools.partial(_attn_kernel, tq=tq, tk=tk, n_heads=n_heads),
        grid=(n_batch, H_A // n_heads, nq),
        in_specs=[
            pl.BlockSpec(memory_space=pltpu.SMEM),
            pl.BlockSpec((tq, gw), lambda b, h, i: (b * nq + i, h)),
            pl.BlockSpec((seq, gw), lambda b, h, i: (b, kcol0 + h)),
            pl.BlockSpec((seq, gw), lambda b, h, i: (b, h)),
            pl.BlockSpec((tq, gw), lambda b, h, i: (meta_blk, kcol0 + h)),
            pl.BlockSpec((tq, gw), lambda b, h, i: (meta_blk, h)),
            pl.BlockSpec((VH_A, 1), lambda b, h, i: (0, 0)),
        ],
        out_specs=pl.BlockSpec((tq, gw), lambda b, h, i: (b * nq + i, h)),
        out_shape=jax.ShapeDtypeStruct((n_prompt, W_A), BF16),
        scratch_shapes=[pltpu.VMEM((seq, gw), BF16), pltpu.VMEM((seq, gw), BF16),
                        pltpu.VMEM((n_heads, 1, 2 * tq), F32), pltpu.VMEM((n_heads, 1, 2 * tq), F32),
                        pltpu.VMEM((n_heads, VH_A, 2 * tq), F32)],
        compiler_params=_params(("arbitrary", "arbitrary", "arbitrary")),
        name="prompt_attention",
    )(sc, qk, qk, z2, qk, z2, g_sub.reshape(VH_A, 1))


def _dec_kernel(pt_ref, sc_ref, q_ref, kn_ref, vn_ref, slc_ref, tcol_ref, hcol_ref, map0_ref, gs_ref, *rest,
                g_pages, n_steps, past, n_new):
    kp_refs = rest[:g_pages]
    vp_refs = rest[g_pages:2 * g_pages]
    o_ref = rest[2 * g_pages]
    m_s, l_s, acc_s = rest[2 * g_pages + 1:]
    j = pl.program_id(1)
    rows = PAGE_SIZE * H_A

    @pl.when(j == 0)
    def _init():
        m_s[...] = jnp.full(m_s.shape, NEG, F32)
        l_s[...] = jnp.zeros(l_s.shape, F32)
        acc_s[...] = jnp.zeros(acc_s.shape, F32)

    qcat = q_ref[...]
    q_hm, q_hi, q_lo = qcat[:, :2 * LANES], qcat[:, :LANES], qcat[:, 2 * LANES:]
    slc = slc_ref[...]
    tcol = tcol_ref[...]
    hcol = hcol_ref[...]

    def row_info(n_rows):
        r = lax.broadcasted_iota(jnp.int32, (n_rows, LANES), 0)
        tok = (r // H_A).astype(F32)
        own_head = (r % H_A).astype(F32) == hcol
        return tok, own_head

    def split(x):
        hi = x.astype(BF16)
        r = x - hi.astype(F32)
        mid = r.astype(BF16)
        return hi, mid, (r - mid.astype(F32)).astype(BF16)

    def halves(x):
        return x[:, :LANES], x[:, LANES:]

    def update(s, pv_fn):
        m_old = m_s[...]
        m_new = jnp.maximum(m_old, jnp.max(s, axis=0, keepdims=True))
        alpha = jnp.exp(m_old - m_new)
        p = jnp.exp(s - m_new)
        l_s[...] = alpha * l_s[...] + jnp.sum(p, axis=0, keepdims=True)
        acc_s[...] = acc_s[...] * alpha + pv_fn(p)
        m_s[...] = m_new

    tok, own_head = row_info(rows)
    rel = tcol - tok

    q_ab = jnp.concatenate([q_hm, q_hm], axis=0)
    q_c = jnp.concatenate([q_lo, q_hi], axis=0)
    for u in range(g_pages):
        k_hi, k_mid, k_lo = split(kp_refs[u][...].reshape(rows, VH_A))
        big, mid = halves(jnp.dot(jnp.concatenate([k_hi, k_mid], axis=1), q_ab, preferred_element_type=F32))
        small = jnp.dot(jnp.concatenate([k_hi, k_lo], axis=1), q_c, preferred_element_type=F32)
        page0 = (past - (j * g_pages + u) * PAGE_SIZE).astype(F32)
        s = big + (mid + small) + (rel + page0) * (-slc)
        s = jnp.where(own_head, s, NEG)

        def pv_page(p, u=u):
            p_hi, p_mid, p_lo = split(p)
            p_hm = jnp.concatenate([p_hi, p_mid], axis=1)
            v_hi, v_mid, v_lo = split(vp_refs[u][...].reshape(rows, VH_A))
            hh, hm = halves(_mm_tn(v_hi, p_hm))
            mh, mm = halves(_mm_tn(v_mid, p_hm))
            small = mm + _mm_tn(v_hi, p_lo) + _mm_tn(v_lo, p_hi)
            return hh + (hm + mh + small)

        update(s, pv_page)

    @pl.when(j == n_steps - 1)
    def _finish():
        qall = q_hi.astype(F32) + qcat[:, LANES:2 * LANES].astype(F32) + q_lo.astype(F32)
        tok_n, own_n = row_info(n_new * H_A)
        s = jnp.dot(kn_ref[...], qall, precision=HI, preferred_element_type=F32)
        s = s + (tcol - tok_n) * (-slc)
        s = jnp.where(jnp.logical_and(own_n, tok_n <= tcol), s, NEG)
        update(s, lambda p: _mm_tn(vn_ref[...], p, True))

        lam = sc_ref[0]
        on = acc_s[...] / l_s[...]
        d = on - lam * pltpu.roll(on, LANES - n_new, axis=1)
        d = jnp.where(map0_ref[...] > 0.5, d, 0.0)
        ms = jnp.mean(d * d, axis=0, keepdims=True)
        o_ref[...] = d * lax.rsqrt(ms + EPS) * gs_ref[...] * (1.0 - LAM_INIT)


def _decode_attention(page_table, sc, qk_s, z2_s, cache_k, cache_v, slopes, g_sub, g_pages=8):
    n_db, n_pages = page_table.shape
    n_sample = qk_s.shape[0]
    n_new = n_sample // n_db
    assert 2 * H_A * n_new == LANES
    past = n_pages * PAGE_SIZE
    n_steps = n_pages // g_pages

    q_s = (qk_s[:, :W_A] * (DH_A ** -0.5)).reshape(n_db, n_new, H_A, 2, DH_A)
    eye_m = jnp.eye(2, dtype=F32)
    qall = jnp.einsum("bthmd,mn->bndhmt", q_s, eye_m).reshape(n_db, VH_A, LANES)
    q_hi = qall.astype(BF16)
    q_res = qall - q_hi.astype(F32)
    q_mid = q_res.astype(BF16)
    qcat = jnp.concatenate([q_hi, q_mid, (q_res - q_mid.astype(F32)).astype(BF16)], axis=-1)
    col = jnp.arange(LANES)
    hcol = (col // (2 * n_new)).astype(F32).reshape(1, LANES)
    slc = slopes[col // (2 * n_new)].reshape(1, LANES)
    tcol = (col % n_new).astype(F32).reshape(1, LANES)
    map0 = ((col // n_new) % 2 == 0).astype(F32).reshape(1, LANES)
    kn = qk_s[:, W_A:].reshape(n_db, n_new * H_A, VH_A)
    vn = z2_s[:, :W_A].reshape(n_db, n_new * H_A, VH_A)

    def page_spec(u):
        return pl.BlockSpec((None, None, PAGE_SIZE, H_A, VH_A),
                            lambda d, j, pt: (0, pt[d * n_pages + j * g_pages + u], 0, 0, 0))

    row = lambda d, j, pt: (0, 0)
    in_specs = [
        pl.BlockSpec(memory_space=pltpu.SMEM),
        pl.BlockSpec((None, VH_A, 3 * LANES), lambda d, j, pt: (d, 0, 0)),
        pl.BlockSpec((None, n_new * H_A, VH_A), lambda d, j, pt: (d, 0, 0)),
        pl.BlockSpec((None, n_new * H_A, VH_A), lambda d, j, pt: (d, 0, 0)),
        pl.BlockSpec((1, LANES), row),
        pl.BlockSpec((1, LANES), row),
        pl.BlockSpec((1, LANES), row),
        pl.BlockSpec((1, LANES), row),
        pl.BlockSpec((VH_A, 1), row),
    ] + [page_spec(u) for u in range(g_pages)] * 2
    grid_spec = pltpu.PrefetchScalarGridSpec(
        num_scalar_prefetch=1,
        grid=(n_db, n_steps),
        in_specs=in_specs,
        out_specs=pl.BlockSpec((None, VH_A, LANES), lambda d, j, pt: (d, 0, 0)),
        scratch_shapes=[pltpu.VMEM((1, LANES), F32), pltpu.VMEM((1, LANES), F32),
                        pltpu.VMEM((VH_A, LANES), F32)],
    )
    out = pl.pallas_call(
        functools.partial(_dec_kernel, g_pages=g_pages, n_steps=n_steps, past=past, n_new=n_new),
        grid_spec=grid_spec,
        out_shape=jax.ShapeDtypeStruct((n_db, VH_A, LANES), F32),
        compiler_params=_params(("arbitrary", "arbitrary")),
        name="decode_attention",
    )(page_table.reshape(-1), sc, qcat, kn, vn, slc, tcol, hcol, map0, g_sub.reshape(VH_A, 1),
      *([cache_k] * g_pages), *([cache_v] * g_pages))
    out = out.reshape(n_db, VH_A, H_A, 2, n_new)[:, :, :, 0, :]
    return jnp.transpose(out, (0, 3, 2, 1)).reshape(n_sample, W_A)


def _gla_kernel(q_ref, k_ref, v_ref, la_ref, r_ref, s0_ref, gg_ref, o_ref, so_ref, state, *,
                chunk, sub, n_chunks, hi):
    state[...] = s0_ref[...]
    row = lax.broadcasted_iota(jnp.int32, (chunk, chunk), 0)
    col = lax.broadcasted_iota(jnp.int32, (chunk, chunk), 1)
    tri = (col <= row).astype(F32)
    tri_sub = (col <= (row // sub) * sub + (sub - 1)).astype(F32)
    causal = col <= row
    colblk = col // sub
    rowc = lax.broadcasted_iota(jnp.int32, (chunk, DK_B), 0)
    ones = jnp.ones((chunk, DK_B), F32)
    n_sub = chunk // sub

    def body(c, carry):
        c0 = pl.multiple_of(c * chunk, chunk)
        sl = pl.ds(c0, chunk)
        la = la_ref[sl, :]
        b = jnp.dot(tri, la, precision=HI, preferred_element_type=F32)
        rsub = jnp.dot(tri_sub, la, precision=HI, preferred_element_type=F32)
        b_end = b[chunk - 1:chunk, :]
        qs = q_ref[sl, :] * (DK_B ** -0.5)
        kk = k_ref[sl, :]
        vv = v_ref[sl, :]
        s_old = state[...]

        inter = _mm(qs * jnp.exp(b), s_old, hi)

        kt = kk * jnp.exp(rsub - b)
        qj = []
        for jb in range(n_sub):
            ref = rsub[jb * sub:jb * sub + 1, :]
            qj.append(qs * jnp.exp(jnp.where(rowc >= jb * sub, b - ref, -jnp.inf)))
        a4 = _mm_nt(jnp.concatenate(qj, axis=0), kt, hi)
        att = a4[:chunk]
        for jb in range(1, n_sub):
            att = jnp.where(colblk == jb, a4[jb * chunk:(jb + 1) * chunk], att)
        att = jnp.where(causal, att, 0.0)
        o = inter + _mm(att, vv, hi)

        dec = jnp.exp(_mm_tn(la, ones, True))
        kd = kk * jnp.exp(b_end - b)
        state[...] = jnp.concatenate([dec] * (DV_B // DK_B), axis=1) * s_old + _mm_tn(kd, vv, hi)

        ms = jnp.mean(o * o, axis=-1, keepdims=True)
        r = r_ref[sl, :]
        y = o * lax.rsqrt(ms + EPS) * gg_ref[...] * (r * jax.nn.sigmoid(r))
        o_ref[sl, :] = y.astype(o_ref.dtype)
        return carry

    lax.fori_loop(0, n_chunks, body, 0)
    so_ref[...] = state[...]


def _gla(z2, la, s0, g_gla, n_batch, seq, row0, chunk, sub, hi, out_dtype, name):
    rb0 = row0 // seq
    qc0 = W_A // DK_B
    kc0 = (W_A + W_KB) // DK_B
    vc0 = (W_A + 2 * W_KB) // DV_B
    rc0 = (W_A + 2 * W_KB + W_B) // DV_B
    per_batch_state = s0.shape[0] != 1
    return pl.pallas_call(
        functools.partial(_gla_kernel, chunk=chunk, sub=sub, n_chunks=seq // chunk, hi=hi),
        grid=(n_batch, H_B),
        in_specs=[
            pl.BlockSpec((seq, DK_B), lambda b, h: (rb0 + b, qc0 + h)),
            pl.BlockSpec((seq, DK_B), lambda b, h: (rb0 + b, kc0 + h)),
            pl.BlockSpec((seq, DV_B), lambda b, h: (rb0 + b, vc0 + h)),
            pl.BlockSpec((seq, DK_B), lambda b, h: (rb0 + b, h)),
            pl.BlockSpec((seq, DV_B), lambda b, h: (rb0 + b, rc0 + h)),
            pl.BlockSpec((None, None, DK_B, DV_B),
                         (lambda b, h: (b, h, 0, 0)) if per_batch_state else (lambda b, h: (0, h, 0, 0))),
            pl.BlockSpec((1, DV_B), lambda b, h: (0, 0)),
        ],
        out_specs=[
            pl.BlockSpec((seq, DV_B), lambda b, h: (b, h)),
            pl.BlockSpec((None, None, DK_B, DV_B), lambda b, h: (b, h, 0, 0)),
        ],
        out_shape=[jax.ShapeDtypeStruct((n_batch * seq, W_B), out_dtype),
                   jax.ShapeDtypeStruct((n_batch, H_B, DK_B, DV_B), F32)],
        scratch_shapes=[pltpu.VMEM((DK_B, DV_B), F32)],
        compiler_params=_params(("arbitrary", "arbitrary")),
        name=name,
    )(z2, z2, z2, la, z2, s0, g_gla.reshape(1, DV_B))


def _merge_kernel(oa_ref, og_ref, h_ref, wa_ref, wb_ref, wga_ref, wgb_ref, o_ref):
    h = h_ref[...]
    pa = jnp.dot(oa_ref[...], wa_ref[...], preferred_element_type=F32)
    ga = jax.nn.sigmoid(jnp.dot(h, wga_ref[...], preferred_element_type=F32))
    acc = ga * pa
    pb = jnp.dot(og_ref[...], wb_ref[...], preferred_element_type=F32)
    gb = jax.nn.sigmoid(jnp.dot(h, wgb_ref[...], preferred_element_type=F32))
    o_ref[...] = (acc + gb * pb).astype(o_ref.dtype)


def _merge(oa, og, h, wa, wb, wg, tm, tn):
    m, k = h.shape
    d = wa.shape[1]
    gb0 = d // tn
    return pl.pallas_call(
        _merge_kernel,
        grid=(d // tn, m // tm),
        in_specs=[
            pl.BlockSpec((tm, W_A), lambda j, i: (i, 0)),
            pl.BlockSpec((tm, W_B), lambda j, i: (i, 0)),
            pl.BlockSpec((tm, k), lambda j, i: (i, 0)),
            pl.BlockSpec((W_A, tn), lambda j, i: (0, j)),
            pl.BlockSpec((W_B, tn), lambda j, i: (0, j)),
            pl.BlockSpec((k, tn), lambda j, i: (0, j)),
            pl.BlockSpec((k, tn), lambda j, i: (0, gb0 + j)),
        ],
        out_specs=pl.BlockSpec((tm, tn), lambda j, i: (i, j)),
        out_shape=jax.ShapeDtypeStruct((m, d), BF16),
        compiler_params=_params(("arbitrary", "arbitrary")),
        name="merge",
    )(oa, og, h, wa, wb, wg, wg)


def _merge_s_kernel(oa_ref, og_ref, wa_ref, wb_ref, ga_ref, gb_ref, o_ref):
    pa = _mm(oa_ref[...], wa_ref[...], True)
    pb = _mm(og_ref[...], wb_ref[...], True)
    o_ref[...] = ga_ref[...] * pa + gb_ref[...] * pb


def _merge_sample(oa, og, wa, wb, sg, tn):
    m = oa.shape[0]
    d = wa.shape[1]
    gb0 = d // tn
    return pl.pallas_call(
        _merge_s_kernel,
        grid=(d // tn,),
        in_specs=[
            pl.BlockSpec((m, W_A), lambda j: (0, 0)),
            pl.BlockSpec((m, W_B), lambda j: (0, 0)),
            pl.BlockSpec((W_A, tn), lambda j: (0, j)),
            pl.BlockSpec((W_B, tn), lambda j: (0, j)),
            pl.BlockSpec((m, tn), lambda j: (0, j)),
            pl.BlockSpec((m, tn), lambda j: (0, gb0 + j)),
        ],
        out_specs=pl.BlockSpec((m, tn), lambda j: (0, j)),
        out_shape=jax.ShapeDtypeStruct((m, d), F32),
        compiler_params=_params(("arbitrary",)),
        name="merge_sample",
    )(oa, og, wa, wb, sg, sg)


def _outproj_kernel(mg_ref, w_ref, x_ref, gf_ref, x1_ref, h2_ref, *, hi):
    x1 = x_ref[...] + _mm(mg_ref[...], w_ref[...], hi)
    x1_ref[...] = x1
    ms = jnp.mean(x1 * x1, axis=-1, keepdims=True)
    h2_ref[...] = (x1 * lax.rsqrt(ms + EPS) * gf_ref[...]).astype(h2_ref.dtype)


def _outproj(mg, w, x, g_ffn, tm, hi, name):
    m, d = x.shape
    return pl.pallas_call(
        functools.partial(_outproj_kernel, hi=hi),
        grid=(m // tm,),
        in_specs=[
            pl.BlockSpec((tm, d), lambda i: (i, 0)),
            pl.BlockSpec((d, d), lambda i: (0, 0)),
            pl.BlockSpec((tm, d), lambda i: (i, 0)),
            pl.BlockSpec((1, d), lambda i: (0, 0)),
        ],
        out_specs=[pl.BlockSpec((tm, d), lambda i: (i, 0)), pl.BlockSpec((tm, d), lambda i: (i, 0))],
        out_shape=[jax.ShapeDtypeStruct((m, d), F32), jax.ShapeDtypeStruct((m, d), F32)],
        compiler_params=_params(("arbitrary",)),
        name=name,
    )(mg, w, x, g_ffn.reshape(1, d))


def _route_kernel(h_ref, w_ref, b_ref, c0_ref, idx_ref, gate_ref, rank_ref, cnt_ref, carry, *, tm, hi):
    @pl.when(pl.program_id(0) == 0)
    def _init():
        carry[...] = c0_ref[...]

    logits = _mm(h_ref[...], w_ref[...], hi) + b_ref[...]
    lane = lax.broadcasted_iota(jnp.int32, (tm, N_EXPERTS), 1).astype(F32)
    work = logits
    vals, idxs, sels = [], [], []
    for _ in range(TOP_K):
        mx = jnp.max(work, axis=-1, keepdims=True)
        ix = jnp.min(jnp.where(work == mx, lane, float(N_EXPERTS)), axis=-1, keepdims=True)
        sel = lane == ix
        vals.append(mx)
        idxs.append(ix)
        sels.append(sel)
        work = jnp.where(sel, -jnp.inf, work)
    es = [jnp.exp(v - vals[0]) for v in vals]
    tot = es[0] + es[1] + es[2] + es[3]
    gate_ref[...] = jnp.concatenate([e / tot for e in es], axis=1)
    idx_ref[...] = jnp.concatenate(idxs, axis=1).astype(jnp.int32)

    onehot = jnp.zeros((tm, N_EXPERTS), F32)
    for sel in sels:
        onehot = onehot + sel.astype(F32)
    r = lax.broadcasted_iota(jnp.int32, (tm, tm), 0)
    c = lax.broadcasted_iota(jnp.int32, (tm, tm), 1)
    before = (c < r).astype(BF16)
    cum = jnp.dot(before, onehot.astype(BF16), preferred_element_type=F32) + carry[...]
    ranks = [jnp.sum(jnp.where(sel, cum, 0.0), axis=-1, keepdims=True) for sel in sels]
    rank_ref[...] = jnp.concatenate(ranks, axis=1).astype(jnp.int32)
    carry[...] = carry[...] + jnp.sum(onehot, axis=0, keepdims=True)
    cnt_ref[...] = carry[...]


def _route(h, n_tok, w_router, b_router, c0, tm, hi, name):
    d = h.shape[1]
    assert n_tok % tm == 0
    return pl.pallas_call(
        functools.partial(_route_kernel, tm=tm, hi=hi),
        grid=(n_tok // tm,),
        in_specs=[
            pl.BlockSpec((tm, d), lambda i: (i, 0)),
            pl.BlockSpec((d, N_EXPERTS), lambda i: (0, 0)),
            pl.BlockSpec((1, N_EXPERTS), lambda i: (0, 0)),
            pl.BlockSpec((1, N_EXPERTS), lambda i: (0, 0)),
        ],
        out_specs=[
            pl.BlockSpec((tm, TOP_K), lambda i: (i, 0)),
            pl.BlockSpec((tm, TOP_K), lambda i: (i, 0)),
            pl.BlockSpec((tm, TOP_K), lambda i: (i, 0)),
            pl.BlockSpec((1, N_EXPERTS), lambda i: (0, 0)),
        ],
        out_shape=[
            jax.ShapeDtypeStruct((n_tok, TOP_K), jnp.int32),
            jax.ShapeDtypeStruct((n_tok, TOP_K), F32),
            jax.ShapeDtypeStruct((n_tok, TOP_K), jnp.int32),
            jax.ShapeDtypeStruct((1, N_EXPERTS), F32),
        ],
        scratch_shapes=[pltpu.VMEM((1, N_EXPERTS), F32)],
        compiler_params=_params(("arbitrary",)),
        name=name,
    )(h, w_router, b_router.reshape(1, N_EXPERTS), c0)


def _up_kernel(be_ref, first_ref, nu_ref, x_ref, wg_ref, wl_ref, bg_ref, bl_ref, o_ref, wgb, wlb):
    b = pl.program_id(1)

    @pl.when(first_ref[b] == 1)
    def _cast():
        wgb[...] = wg_ref[...].astype(BF16)
        wlb[...] = wl_ref[...].astype(BF16)

    @pl.when(b < nu_ref[0])
    def _compute():
        x = x_ref[...]
        g = jnp.dot(x, wgb[...], preferred_element_type=F32) + bg_ref[...]
        lin = jnp.dot(x, wlb[...], preferred_element_type=F32) + bl_ref[...]
        g = jnp.minimum(g, SWIGLU_LIMIT)
        lin = jnp.clip(lin, -SWIGLU_LIMIT, SWIGLU_LIMIT)
        glu = g * jax.nn.sigmoid(SWIGLU_ALPHA * g)
        o_ref[...] = ((lin + 1.0) * glu).astype(o_ref.dtype)

    @pl.when(b >= nu_ref[0])
    def _unused():
        o_ref[...] = jnp.zeros(o_ref.shape, o_ref.dtype)


def _expert_up(be, first, nu, xbuf, w_up, b_up, tm, th):
    rows, d = xbuf.shape
    d_e = w_up.shape[-1] // 2
    nj = d_e // th
    grid_spec = pltpu.PrefetchScalarGridSpec(
        num_scalar_prefetch=3,
        grid=(nj, rows // tm),
        in_specs=[
            pl.BlockSpec((tm, d), lambda j, b, be, fi, nu: (b, 0)),
            pl.BlockSpec((None, None, d, th), lambda j, b, be, fi, nu: (0, be[b], 0, j)),
            pl.BlockSpec((None, None, d, th), lambda j, b, be, fi, nu: (0, be[b], 0, nj + j)),
            pl.BlockSpec((None, None, 1, th), lambda j, b, be, fi, nu: (0, be[b], 0, j)),
            pl.BlockSpec((None, None, 1, th), lambda j, b, be, fi, nu: (0, be[b], 0, nj + j)),
        ],
        out_specs=pl.BlockSpec((tm, th), lambda j, b, be, fi, nu: (b, j)),
        scratch_shapes=[pltpu.VMEM((d, th), BF16), pltpu.VMEM((d, th), BF16)],
    )
    return pl.pallas_call(
        _up_kernel,
        grid_spec=grid_spec,
        out_shape=jax.ShapeDtypeStruct((rows, d_e), BF16),
        compiler_params=_params(("arbitrary", "arbitrary")),
        name="expert_up",
    )(be, first, nu, xbuf, w_up, w_up, b_up, b_up)


def _down_kernel(be_ref, first_ref, nu_ref, a_ref, w_ref, b_ref, o_ref, wb):
    b = pl.program_id(1)

    @pl.when(first_ref[b] == 1)
    def _cast():
        wb[...] = w_ref[...].astype(BF16)

    @pl.when(b < nu_ref[0])
    def _compute():
        o_ref[...] = jnp.dot(a_ref[...], wb[...], preferred_element_type=F32) + b_ref[...]

    @pl.when(b >= nu_ref[0])
    def _unused():
        o_ref[...] = jnp.zeros(o_ref.shape, o_ref.dtype)


def _expert_down(be, first, nu, act, w_down, b_down, tm, tn):
    rows, d_e = act.shape
    d = w_down.shape[-1]
    grid_spec = pltpu.PrefetchScalarGridSpec(
        num_scalar_prefetch=3,
        grid=(d // tn, rows // tm),
        in_specs=[
            pl.BlockSpec((tm, d_e), lambda j, b, be, fi, nu: (b, 0)),
            pl.BlockSpec((None, None, d_e, tn), lambda j, b, be, fi, nu: (0, be[b], 0, j)),
            pl.BlockSpec((None, None, 1, tn), lambda j, b, be, fi, nu: (0, be[b], 0, j)),
        ],
        out_specs=pl.BlockSpec((tm, tn), lambda j, b, be, fi, nu: (b, j)),
        scratch_shapes=[pltpu.VMEM((d_e, tn), BF16)],
    )
    return pl.pallas_call(
        _down_kernel,
        grid_spec=grid_spec,
        out_shape=jax.ShapeDtypeStruct((rows, d), F32),
        compiler_params=_params(("arbitrary", "arbitrary")),
        name="expert_down",
    )(be, first, nu, act, w_down, b_down)


def _row_copy(src_hbm, row, dst, sem):
    return pltpu.make_async_copy(src_hbm.at[pl.ds(row, 1)], dst, sem)


def _dispatch_kernel(tok_ref, h_hbm, o_ref, buf, sem, *, tm):
    b = pl.program_id(0)

    def start_tile(blk, slot):
        def body(r, carry):
            _row_copy(h_hbm, tok_ref[blk * tm + r], buf.at[slot, pl.ds(r, 1)], sem.at[slot]).start()
            return carry
        lax.fori_loop(0, tm, body, 0, unroll=8)

    @pl.when(b == 0)
    def _first():
        start_tile(0, 0)

    @pl.when(b + 1 < pl.num_programs(0))
    def _next():
        start_tile(b + 1, (b + 1) % 2)

    slot = b % 2

    def wait_body(r, carry):
        _row_copy(h_hbm, 0, buf.at[slot, pl.ds(r, 1)], sem.at[slot]).wait()
        return carry
    lax.fori_loop(0, tm, wait_body, 0, unroll=8)
    o_ref[...] = buf[slot].astype(o_ref.dtype)


def _dispatch(tok_of_row, h, tm):
    rows = tok_of_row.shape[0]
    d = h.shape[1]
    grid_spec = pltpu.PrefetchScalarGridSpec(
        num_scalar_prefetch=1,
        grid=(rows // tm,),
        in_specs=[pl.BlockSpec(memory_space=pl.ANY)],
        out_specs=pl.BlockSpec((tm, d), lambda b, tok: (b, 0)),
        scratch_shapes=[pltpu.VMEM((2, tm, d), F32), pltpu.SemaphoreType.DMA((2,))],
    )
    return pl.pallas_call(
        functools.partial(_dispatch_kernel, tm=tm),
        grid_spec=grid_spec,
        out_shape=jax.ShapeDtypeStruct((rows, d), BF16),
        compiler_params=_params(("arbitrary",)),
        name="moe_dispatch",
    )(tok_of_row, h)


def _combine_kernel(dest_ref, y_hbm, x_ref, gate_ref, o_ref, buf, sem, *, tm):
    i = pl.program_id(0)

    def start_tile(blk, slot):
        def body(r, carry):
            for k in range(TOP_K):
                row = dest_ref[(blk * tm + r) * TOP_K + k]
                _row_copy(y_hbm, row, buf.at[slot, k, pl.ds(r, 1)], sem.at[slot]).start()
            return carry
        lax.fori_loop(0, tm, body, 0, unroll=4)

    @pl.when(i == 0)
    def _first():
        start_tile(0, 0)

    @pl.when(i + 1 < pl.num_programs(0))
    def _next():
        start_tile(i + 1, (i + 1) % 2)

    slot = i % 2

    def wait_body(r, carry):
        for k in range(TOP_K):
            _row_copy(y_hbm, 0, buf.at[slot, k, pl.ds(r, 1)], sem.at[slot]).wait()
        return carry
    lax.fori_loop(0, tm, wait_body, 0, unroll=4)
    g = gate_ref[...]
    y = g[:, 0:1] * buf[slot, 0]
    for k in range(1, TOP_K):
        y = y + g[:, k:k + 1] * buf[slot, k]
    o_ref[...] = x_ref[...] + y


def _combine(dest, ybuf, x1, gate, tm):
    n_tok, d = x1.shape
    assert n_tok % tm == 0
    grid_spec = pltpu.PrefetchScalarGridSpec(
        num_scalar_prefetch=1,
        grid=(n_tok // tm,),
        in_specs=[pl.BlockSpec(memory_space=pl.ANY),
                  pl.BlockSpec((tm, d), lambda i, dest: (i, 0)),
                  pl.BlockSpec((tm, TOP_K), lambda i, dest: (i, 0))],
        out_specs=pl.BlockSpec((tm, d), lambda i, dest: (i, 0)),
        scratch_shapes=[pltpu.VMEM((2, TOP_K, tm, d), F32), pltpu.SemaphoreType.DMA((2,))],
    )
    return pl.pallas_call(
        functools.partial(_combine_kernel, tm=tm),
        grid_spec=grid_spec,
        out_shape=jax.ShapeDtypeStruct((n_tok, d), F32),
        compiler_params=_params(("arbitrary",)),
        name="moe_combine",
    )(dest.reshape(-1), ybuf, x1, gate)


def kernel(x_prompt, x_sample, cache_k, cache_v, state_gla, page_table, meta_tokens, g_mix, w_in, g_q, g_k, lam_q1, lam_k1, lam_q2, lam_k2, g_sub, w_fg, b_fg, g_gla, w_br_a, w_br_b, w_out, g_ffn, w_router, b_router, w_e_up, b_e_up, w_e_down, b_e_down):
    assert w_in.shape[0] == 1, "single-layer trunk"
    n_batch, seq, d = x_prompt.shape
    n_db, n_new, _ = x_sample.shape
    assert 2 * H_A * n_new == LANES
    n_prompt = n_batch * seq
    n_sample = n_db * n_new
    n_tok = n_prompt + n_sample
    row_m = n_prompt

    def layer0(x):
        return x.reshape(x.shape[1:])

    w = layer0(w_in)
    w_a, w_b, w_o, w_r = layer0(w_br_a), layer0(w_br_b), layer0(w_out), layer0(w_router)
    lam =(jnp.exp(jnp.sum(lam_q1[0] * lam_k1[0])) - jnp.exp(jnp.sum(lam_q2[0] * lam_k2[0])) + LAM_INIT)
    slopes = jnp.exp2(-8.0 * jnp.arange(1, H_A + 1, dtype=F32) / H_A)
    sc = jnp.concatenate([lam.reshape(1), slopes]).astype(F32)
    g_qk = jnp.stack([g_q[0].reshape(1, VH_A), g_k[0].reshape(1, VH_A)])
    wfg_pad = jnp.zeros((LANES, W_KB), F32).at[:GLA_RANK].set(w_fg[0])
    bfg = b_fg[0].reshape(1, W_KB)
    w_gates = w[:, COL_G:]

    pad_rows = (-(n_prompt + N_META)) % (5 * 13 * LANES)
    xm = jnp.concatenate([x_prompt.reshape(n_prompt, d), meta_tokens.astype(F32),
                          jnp.zeros((pad_rows, d), F32)], axis=0)
    n_rows = xm.shape[0]
    tm_proj = n_rows // 5
    tm_row = n_rows // 13
    hm = _rms(xm, g_mix[0], BF16, tm=tm_row)
    qk = _proj(hm, w_in, COL_QK, 2 * W_A, 512, tm_proj, "qk", False, (g_qk,), name="proj_qk")
    z2 = _proj(hm, w_in, COL_Z2, Z2_COLS, 512, tm_proj, "plain", False, name="proj_z2")
    la = _proj(hm, w_in, COL_A, LANES, LANES, tm_proj, "loga", False, (wfg_pad, bfg), name="proj_loga")

    oa_p = _prompt_attention(sc, qk, z2, g_sub[0], n_batch, seq, row_m)
    zero_state = jnp.zeros((1, H_B, DK_B, DV_B), F32)
    _, s_meta = _gla(z2, la, zero_state, g_gla[0], 1, N_META, row_m, N_META, N_META, False, BF16, "gla_meta")
    og_p, s_prompt = _gla(z2, la, s_meta, g_gla[0], n_batch, seq, 0, GLA_CHUNK, GLA_SUB, False, BF16,
                          "gla_prompt")
    tail = jnp.zeros((n_rows - n_prompt, W_A), BF16)
    mg = _merge(jnp.concatenate([oa_p, tail], axis=0), jnp.concatenate([og_p, tail], axis=0), hm,
                w_a.astype(BF16), w_b.astype(BF16), w_gates.astype(BF16), tm_row, 512)
    x1_m, h2_m = _outproj(mg, w_o.astype(BF16), xm, g_ffn[0], tm_row // 2, False, "outproj")

    xs = x_sample.reshape(n_sample, d)
    hs = _rms(xs, g_mix[0], F32, tm=n_sample)
    qk_s = _proj(hs, w_in, COL_QK, 2 * W_A, 512, n_sample, "qk", True, (g_qk,), name="proj_qk_s")
    z2_s = _proj(hs, w_in, COL_Z2, Z2_COLS, 512, n_sample, "plain", True, name="proj_z2_s")
    la_s = _proj(hs, w_in, COL_A, LANES, LANES, n_sample, "loga", True, (wfg_pad, bfg), name="proj_loga_s")
    sg_s = _proj(hs, w_gates, 0, 2 * d, 512, n_sample, "gate", True, name="proj_gate_s")

    oa_s = _decode_attention(page_table, sc, qk_s, z2_s, cache_k, cache_v, slopes, g_sub[0])
    og_s, s_sample = _gla(z2_s, la_s, layer0(state_gla), g_gla[0], n_db, n_new, 0, n_new, n_new, True, F32,
                          "gla_sample")
    mg_s = _merge_sample(oa_s, og_s, w_a, w_b, sg_s, 512)
    x1_s, h2_s = _outproj(mg_s, w_o, xs, g_ffn[0], n_sample, True, "outproj_sample")

    c0 = jnp.zeros((1, N_EXPERTS), F32)
    idx_m, gate_m, rank_m, cnt_m = _route(h2_m, n_prompt, w_r, b_router[0], c0, 512, False, "route_prompt")
    idx_s, gate_s, rank_s, cnt = _route(h2_s, n_sample, w_r, b_router[0], cnt_m, n_sample, True,
                                        "route_sample")
    eidx = jnp.concatenate([idx_m, idx_s], axis=0)
    gate = jnp.concatenate([gate_m, gate_s], axis=0)
    rank = jnp.concatenate([rank_m, rank_s], axis=0)
    h2 = jnp.concatenate([h2_m[:n_prompt], h2_s], axis=0)
    x1 = jnp.concatenate([x1_m[:n_prompt], x1_s], axis=0)

    tme = 256
    n_blocks = -(-(n_tok * TOP_K) // tme) + N_EXPERTS
    counts = cnt.reshape(N_EXPERTS).astype(jnp.int32)
    padded = (counts + tme - 1) // tme * tme
    pend = jnp.cumsum(padded)
    pstart = pend - padded
    dest = pstart[eidx] + rank
    n_used = pend[-1] // tme
    blk = jnp.arange(n_blocks, dtype=jnp.int32)
    be = jnp.minimum(jnp.sum(pend[None, :] <= (blk * tme)[:, None], axis=1), N_EXPERTS - 1).astype(jnp.int32)
    be = jnp.where(blk < n_used, be, be[jnp.maximum(n_used - 1, 0)])
    first = jnp.concatenate([jnp.ones((1,), jnp.int32), (be[1:] != be[:-1]).astype(jnp.int32)])
    nu = n_used.reshape(1).astype(jnp.int32)
    tok = jnp.repeat(jnp.arange(n_tok, dtype=jnp.int32), TOP_K)
    tok_of_row = (jnp.arange(n_blocks * tme, dtype=jnp.int32) % n_tok).at[dest.reshape(-1)].set(tok)

    xbuf = _dispatch(tok_of_row, h2, tme)
    act = _expert_up(be, first, nu, xbuf, w_e_up, b_e_up.reshape(1, N_EXPERTS, 1, -1), tme, 1024)
    ybuf = _expert_down(be, first, nu, act, w_e_down, b_e_down.reshape(1, N_EXPERTS, 1, -1), tme, 1024)
    out = _combine(dest, ybuf, x1, gate, 64)

    y_prompt = out[:n_prompt].reshape(n_batch, seq, d)
    y_sample = out[n_prompt:].reshape(n_db, n_new, d)
    k_meta = jnp.broadcast_to(qk[row_m:row_m + N_META, W_A:][None], (n_batch, N_META, W_A))
    v_meta = jnp.broadcast_to(z2[row_m:row_m + N_META, :W_A][None], (n_batch, N_META, W_A))
    k_prompt = jnp.concatenate([k_meta, qk[:n_prompt, W_A:].reshape(n_batch, seq, W_A)], axis=1)
    v_prompt = jnp.concatenate([v_meta, z2[:n_prompt, :W_A].reshape(n_batch, seq, W_A)], axis=1)
    k_prompt = k_prompt.reshape(1, n_batch, seq + N_META, H_A, VH_A)
    v_prompt = v_prompt.reshape(1, n_batch, seq + N_META, H_A, VH_A)
    k_sample = qk_s[:, W_A:].reshape(1, n_db, n_new, H_A, VH_A)
    v_sample = z2_s[:, :W_A].reshape(1, n_db, n_new, H_A, VH_A)
    return (y_prompt, y_sample, k_prompt, v_prompt, s_prompt[None], k_sample, v_sample, s_sample[None])
```

```python
import functools
import math

import jax
import jax.numpy as jnp
from jax import lax
from jax.experimental import pallas as pl
from jax.experimental.pallas import tpu as pltpu

F32 = jnp.float32
BF16 = jnp.bfloat16
HI = lax.Precision.HIGHEST

N_META = 16
H_A = 8
DH_A = 64
VH_A = 2 * DH_A
W_A = H_A * VH_A
H_B = 4
DK_B = 128
DV_B = 256
W_KB = H_B * DK_B
W_B = H_B * DV_B
GLA_RANK = 16
GLA_TAU = 16.0
GLA_CHUNK = 64
GLA_SUB = 16
N_EXPERTS = 32
TOP_K = 4
SWIGLU_LIMIT = 7.0
SWIGLU_ALPHA = 1.702
PAGE_SIZE = 128
EPS = 1e-6
LAM_INIT = 0.8 - 0.6 * math.exp(-0.3 * 0)

COL_QK = 0
COL_Z2 = 2 * W_A
Z2_COLS = W_A + 2 * W_KB + 2 * W_B
COL_A = COL_Z2 + Z2_COLS
COL_G = COL_A + GLA_RANK

LANES = 128
VMEM_LIMIT = 56 * 1024 * 1024

NEG = -1e30


def _mm(a, b, hi=False):
    if hi:
        return jnp.dot(a, b, precision=HI, preferred_element_type=F32)
    return jnp.dot(a.astype(BF16), b.astype(BF16), preferred_element_type=F32)


def _mm_nt(a, b, hi=False):
    dn = (((1,), (1,)), ((), ()))
    if hi:
        return lax.dot_general(a, b, dn, precision=HI, preferred_element_type=F32)
    return lax.dot_general(a.astype(BF16), b.astype(BF16), dn, preferred_element_type=F32)


def _mm_tn(a, b, hi=False):
    dn = (((0,), (0,)), ((), ()))
    if hi:
        return lax.dot_general(a, b, dn, precision=HI, preferred_element_type=F32)
    return lax.dot_general(a.astype(BF16), b.astype(BF16), dn, preferred_element_type=F32)


def _params(sem, vmem=VMEM_LIMIT):
    return pltpu.CompilerParams(dimension_semantics=sem, vmem_limit_bytes=vmem)


def _rms_kernel(x_ref, g_ref, o_ref):
    x = x_ref[...]
    ms = jnp.mean(x * x, axis=-1, keepdims=True)
    o_ref[...] = (x * lax.rsqrt(ms + EPS) * g_ref[...]).astype(o_ref.dtype)


def _rms(x, g, out_dtype, tm):
    m, d = x.shape
    return pl.pallas_call(
        _rms_kernel,
        grid=(m // tm,),
        in_specs=[pl.BlockSpec((tm, d), lambda i: (i, 0)),
                  pl.BlockSpec((1, d), lambda i: (0, 0))],
        out_specs=pl.BlockSpec((tm, d), lambda i: (i, 0)),
        out_shape=jax.ShapeDtypeStruct((m, d), out_dtype),
        compiler_params=_params(("arbitrary",)),
        name="rms_rows",
    )(x, g.reshape(1, d))


def _proj_kernel(*refs, mode, hi, tn):
    h_ref, w_ref = refs[0], refs[1]
    if mode == "qk":
        g_ref, o_ref = refs[2], refs[3]
        scr = refs[4:]
    elif mode == "loga":
        wfg_ref, bfg_ref, o_ref = refs[2], refs[3], refs[4]
        scr = refs[5:]
    else:
        o_ref = refs[2]
        scr = refs[3:]

    if hi:
        z = jnp.dot(h_ref[...], w_ref[...], precision=HI, preferred_element_type=F32)
    else:
        wbf = scr[0]

        @pl.when(pl.program_id(1) == 0)
        def _cast():
            wbf[...] = w_ref[...].astype(BF16)

        z = jnp.dot(h_ref[...], wbf[...], preferred_element_type=F32)

    if mode == "plain":
        o_ref[...] = z
    elif mode == "gate":
        o_ref[...] = jax.nn.sigmoid(z)
    elif mode == "qk":
        lane = lax.broadcasted_iota(jnp.int32, (z.shape[0], LANES), 1)
        lo = lane < DH_A
        g = g_ref[0]
        for c in range(tn // LANES):
            x = z[:, c * LANES:(c + 1) * LANES]
            x2 = x * x
            s0 = jnp.sum(jnp.where(lo, x2, 0.0), axis=-1, keepdims=True)
            s1 = jnp.sum(jnp.where(lo, 0.0, x2), axis=-1, keepdims=True)
            inv = jnp.where(lo, lax.rsqrt(s0 * (1.0 / DH_A) + EPS),
                            lax.rsqrt(s1 * (1.0 / DH_A) + EPS))
            o_ref[:, c * LANES:(c + 1) * LANES] = x * inv * g
    else:
        lane = lax.broadcasted_iota(jnp.int32, z.shape, 1)
        a = jnp.where(lane < GLA_RANK, z, 0.0)
        u = _mm(a, wfg_ref[...], hi) + bfg_ref[...]
        o_ref[...] = -(jnp.maximum(-u, 0.0) + jnp.log1p(jnp.exp(-jnp.abs(u)))) * (1.0 / GLA_TAU)


def _proj(h, w, col0, n_cols, tn, tm, mode, hi, extras=(), name="proj"):
    m, k = h.shape
    assert col0 % tn == 0 and n_cols % tn == 0 and m % tm == 0
    cb0 = col0 // tn
    if w.ndim == 3:
        w_spec = pl.BlockSpec((None, k, tn), lambda j, i: (0, 0, cb0 + j))
    else:
        w_spec = pl.BlockSpec((k, tn), lambda j, i: (0, cb0 + j))
    in_specs = [pl.BlockSpec((tm, k), lambda j, i: (i, 0)), w_spec]
    out_cols, out_tn = n_cols, tn
    if mode == "qk":
        in_specs.append(pl.BlockSpec((1, 1, LANES), lambda j, i: ((j * tn) // W_A, 0, 0)))
    elif mode == "loga":
        in_specs += [pl.BlockSpec((LANES, W_KB), lambda j, i: (0, 0)),
                     pl.BlockSpec((1, W_KB), lambda j, i: (0, 0))]
        out_cols, out_tn = W_KB, W_KB
    return pl.pallas_call(
        functools.partial(_proj_kernel, mode=mode, hi=hi, tn=tn),
        grid=(n_cols // tn, m // tm),
        in_specs=in_specs,
        out_specs=pl.BlockSpec((tm, out_tn), lambda j, i: (i, j)),
        out_shape=jax.ShapeDtypeStruct((m, out_cols), F32),
        scratch_shapes=[] if hi else [pltpu.VMEM((k, tn), BF16)],
        compiler_params=_params(("arbitrary", "arbitrary")),
        name=name,
    )(h, w, *extras)


def _attn_kernel(sc_ref, q_ref, k_ref, v_ref, km_ref, vm_ref, gs_ref, o_ref, kbf, vbf, m_s, l_s, acc_s, *,
                 tq, tk, n_heads):
    hg = pl.program_id(1)
    i = pl.program_id(2)
    half = tk // 2
    assert half == tq

    @pl.when(i == 0)
    def _cast():
        kbf[...] = k_ref[...].astype(BF16)
        vbf[...] = v_ref[...].astype(BF16)

    lam = sc_ref[0]
    q0 = i * tq
    width = 2 * tq
    lane = lax.broadcasted_iota(jnp.int32, (tq, LANES), 1)
    lo = lane < DH_A

    def tile_rel(n_keys):
        key = lax.broadcasted_iota(jnp.int32, (n_keys, width), 0)
        qry = lax.broadcasted_iota(jnp.int32, (n_keys, width), 1) % tq
        return key, qry, (qry - key).astype(F32)

    _, _, rel_full = tile_rel(tk)
    key, qry, rel_half = tile_rel(tq)

    def head_cols(g):
        return slice(g * LANES, (g + 1) * LANES)

    slopes, q12s = [], []
    for g in range(n_heads):
        slopes.append(sc_ref[1 + hg * n_heads + g])
        q = q_ref[:, head_cols(g)] * (DH_A ** -0.5)
        q12s.append(jnp.concatenate([jnp.where(lo, q, 0.0), jnp.where(lo, 0.0, q)], axis=0).astype(BF16))

    def scores(g, kc, n_keys, k0):
        s = lax.dot_general(kc, q12s[g], (((1,), (1,)), ((), ())), preferred_element_type=F32)
        rel = rel_full if n_keys == tk else rel_half
        return s + (rel + (q0 - k0).astype(F32)) * (-slopes[g])

    def update(g, s, vc):
        m = m_s[g]
        m_new = jnp.maximum(m, jnp.max(s, axis=0, keepdims=True))
        alpha = jnp.exp(m - m_new)
        p = jnp.exp(s - m_new)
        l_s[g] = alpha * l_s[g] + jnp.sum(p, axis=0, keepdims=True)
        acc_s[g] = alpha * acc_s[g] + _mm_tn(vc, p)
        m_s[g] = m_new

    for g in range(n_heads):
        s = scores(g, km_ref[:, head_cols(g)].astype(BF16), tq, -N_META)
        s = jnp.where(key < N_META, s, NEG)
        m0 = jnp.max(s, axis=0, keepdims=True)
        p = jnp.exp(s - m0)
        m_s[g] = m0
        l_s[g] = jnp.sum(p, axis=0, keepdims=True)
        acc_s[g] = _mm_tn(vm_ref[:, head_cols(g)], p)

    def full_chunk(j, carry):
        k0 = pl.multiple_of(j * tk, tk)
        for g in range(n_heads):
            update(g, scores(g, kbf[pl.ds(k0, tk), head_cols(g)], tk, k0), vbf[pl.ds(k0, tk), head_cols(g)])
        return carry

    def half_chunk(j, carry):
        k0 = pl.multiple_of(q0 - half, half)
        for g in range(n_heads):
            update(g, scores(g, kbf[pl.ds(k0, half), head_cols(g)], half, k0),
                   vbf[pl.ds(k0, half), head_cols(g)])
        return carry

    lax.fori_loop(0, q0 // tk, full_chunk, 0)
    lax.fori_loop(0, i % 2, half_chunk, 0)
    kq = pl.multiple_of(q0, tq)
    for g in range(n_heads):
        s = scores(g, kbf[pl.ds(kq, tq), head_cols(g)], tq, kq)
        s = jnp.where(qry >= key, s, NEG)
        update(g, s, vbf[pl.ds(kq, tq), head_cols(g)])
        accn = acc_s[g] / l_s[g]
        o = accn[:, :tq] - lam * accn[:, tq:]
        ms = jnp.mean(o * o, axis=0, keepdims=True)
        y = o * lax.rsqrt(ms + EPS) * gs_ref[...] * (1.0 - LAM_INIT)
        o_ref[:, head_cols(g)] = y.T.astype(o_ref.dtype)


def _prompt_attention(sc, qk, z2, g_sub, n_batch, seq, meta_row0, tq=128, tk=256, n_heads=8):
    nq = seq // tq
    n_prompt = n_batch * seq
    assert meta_row0 % tq == 0 and qk.shape[0] >= meta_row0 + tq and H_A % n_heads == 0
    meta_blk = meta_row0 // tq
    gw = n_heads * LANES
    kcol0 = W_A // gw
    return pl.pallas_call(
        functools.partial(_attn_kernel, tq=tq, tk=tk, n_heads=n_heads),
        grid=(n_batch, H_A // n_heads, nq),
        in_specs=[
            pl.BlockSpec(memory_space=pltpu.SMEM),
            pl.BlockSpec((tq, gw), lambda b, h, i: (b * nq + i, h)),
            pl.BlockSpec((seq, gw), lambda b, h, i: (b, kcol0 + h)),
            pl.BlockSpec((seq, gw), lambda b, h, i: (b, h)),
            pl.BlockSpec((tq, gw), lambda b, h, i: (meta_blk, kcol0 + h)),
            pl.BlockSpec((tq, gw), lambda b, h, i: (meta_blk, h)),
            pl.BlockSpec((VH_A, 1), lambda b, h, i: (0, 0)),
        ],
        out_specs=pl.BlockSpec((tq, gw), lambda b, h, i: (b * nq + i, h)),
        out_shape=jax.ShapeDtypeStruct((n_prompt, W_A), BF16),
        scratch_shapes=[pltpu.VMEM((seq, gw), BF16), pltpu.VMEM((seq, gw), BF16),
                        pltpu.VMEM((n_heads, 1, 2 * tq), F32), pltpu.VMEM((n_heads, 1, 2 * tq), F32),
                        pltpu.VMEM((n_heads, VH_A, 2 * tq), F32)],
        compiler_params=_params(("arbitrary", "arbitrary", "arbitrary")),
        name="prompt_attention",
    )(sc, qk, qk, z2, qk, z2, g_sub.reshape(VH_A, 1))


def _dec_kernel(pt_ref, sc_ref, q_ref, kn_ref, vn_ref, slc_ref, tcol_ref, hcol_ref, map0_ref, gs_ref, *rest,
                g_pages, n_steps, past, n_new):
    kp_refs = rest[:g_pages]
    vp_refs = rest[g_pages:2 * g_pages]
    o_ref = rest[2 * g_pages]
    m_s, l_s, acc_s = rest[2 * g_pages + 1:]
    j = pl.program_id(1)
    rows = PAGE_SIZE * H_A

    @pl.when(j == 0)
    def _init():
        m_s[...] = jnp.full(m_s.shape, NEG, F32)
        l_s[...] = jnp.zeros(l_s.shape, F32)
        acc_s[...] = jnp.zeros(acc_s.shape, F32)

    qcat = q_ref[...]
    q_hm, q_hi, q_lo = qcat[:, :2 * LANES], qcat[:, :LANES], qcat[:, 2 * LANES:]
    slc = slc_ref[...]
    tcol = tcol_ref[...]
    hcol = hcol_ref[...]

    def row_info(n_rows):
        r = lax.broadcasted_iota(jnp.int32, (n_rows, LANES), 0)
        tok = (r // H_A).astype(F32)
        own_head = (r % H_A).astype(F32) == hcol
        return tok, own_head

    def split(x):
        hi = x.astype(BF16)
        r = x - hi.astype(F32)
        mid = r.astype(BF16)
        return hi, mid, (r - mid.astype(F32)).astype(BF16)

    def halves(x):
        return x[:, :LANES], x[:, LANES:]

    def update(s, pv_fn):
        m_old = m_s[...]
        m_new = jnp.maximum(m_old, jnp.max(s, axis=0, keepdims=True))
        alpha = jnp.exp(m_old - m_new)
        p = jnp.exp(s - m_new)
        l_s[...] = alpha * l_s[...] + jnp.sum(p, axis=0, keepdims=True)
        acc_s[...] = acc_s[...] * alpha + pv_fn(p)
        m_s[...] = m_new

    tok, own_head = row_info(rows)
    rel = tcol - tok

    q_ab = jnp.concatenate([q_hm, q_hm], axis=0)
    q_c = jnp.concatenate([q_lo, q_hi], axis=0)
    for u in range(g_pages):
        k_hi, k_mid, k_lo = split(kp_refs[u][...].reshape(rows, VH_A))
        big, mid = halves(jnp.dot(jnp.concatenate([k_hi, k_mid], axis=1), q_ab, preferred_element_type=F32))
        small = jnp.dot(jnp.concatenate([k_hi, k_lo], axis=1), q_c, preferred_element_type=F32)
        page0 = (past - (j * g_pages + u) * PAGE_SIZE).astype(F32)
        s = big + (mid + small) + (rel + page0) * (-slc)
        s = jnp.where(own_head, s, NEG)

        def pv_page(p, u=u):
            p_hi, p_mid, p_lo = split(p)
            p_hm = jnp.concatenate([p_hi, p_mid], axis=1)
            v_hi, v_mid, v_lo = split(vp_refs[u][...].reshape(rows, VH_A))
            hh, hm = halves(_mm_tn(v_hi, p_hm))
            mh, mm = halves(_mm_tn(v_mid, p_hm))
            small = mm + _mm_tn(v_hi, p_lo) + _mm_tn(v_lo, p_hi)
            return hh + (hm + mh + small)

        update(s, pv_page)

    @pl.when(j == n_steps - 1)
    def _finish():
        qall = q_hi.astype(F32) + qcat[:, LANES:2 * LANES].astype(F32) + q_lo.astype(F32)
        tok_n, own_n = row_info(n_new * H_A)
        s = jnp.dot(kn_ref[...], qall, precision=HI, preferred_element_type=F32)
        s = s + (tcol - tok_n) * (-slc)
        s = jnp.where(jnp.logical_and(own_n, tok_n <= tcol), s, NEG)
        update(s, lambda p: _mm_tn(vn_ref[...], p, True))

        lam = sc_ref[0]
        on = acc_s[...] / l_s[...]
        d = on - lam * pltpu.roll(on, LANES - n_new, axis=1)
        d = jnp.where(map0_ref[...] > 0.5, d, 0.0)
        ms = jnp.mean(d * d, axis=0, keepdims=True)
        o_ref[...] = d * lax.rsqrt(ms + EPS) * gs_ref[...] * (1.0 - LAM_INIT)


def _decode_attention(page_table, sc, qk_s, z2_s, cache_k, cache_v, slopes, g_sub, g_pages=8):
    n_db, n_pages = page_table.shape
    n_sample = qk_s.shape[0]
    n_new = n_sample // n_db
    assert 2 * H_A * n_new == LANES
    past = n_pages * PAGE_SIZE
    n_steps = n_pages // g_pages

    q_s = (qk_s[:, :W_A] * (DH_A ** -0.5)).reshape(n_db, n_new, H_A, 2, DH_A)
    eye_m = jnp.eye(2, dtype=F32)
    qall = jnp.einsum("bthmd,mn->bndhmt", q_s, eye_m).reshape(n_db, VH_A, LANES)
    q_hi = qall.astype(BF16)
    q_res = qall - q_hi.astype(F32)
    q_mid = q_res.astype(BF16)
    qcat = jnp.concatenate([q_hi, q_mid, (q_res - q_mid.astype(F32)).astype(BF16)], axis=-1)
    col = jnp.arange(LANES)
    hcol = (col // (2 * n_new)).astype(F32).reshape(1, LANES)
    slc = slopes[col // (2 * n_new)].reshape(1, LANES)
    tcol = (col % n_new).astype(F32).reshape(1, LANES)
    map0 = ((col // n_new) % 2 == 0).astype(F32).reshape(1, LANES)
    kn = qk_s[:, W_A:].reshape(n_db, n_new * H_A, VH_A)
    vn = z2_s[:, :W_A].reshape(n_db, n_new * H_A, VH_A)

    def page_spec(u):
        return pl.BlockSpec((None, None, PAGE_SIZE, H_A, VH_A),
                            lambda d, j, pt: (0, pt[d * n_pages + j * g_pages + u], 0, 0, 0))

    row = lambda d, j, pt: (0, 0)
    in_specs = [
        pl.BlockSpec(memory_space=pltpu.SMEM),
        pl.BlockSpec((None, VH_A, 3 * LANES), lambda d, j, pt: (d, 0, 0)),
        pl.BlockSpec((None, n_new * H_A, VH_A), lambda d, j, pt: (d, 0, 0)),
        pl.BlockSpec((None, n_new * H_A, VH_A), lambda d, j, pt: (d, 0, 0)),
        pl.BlockSpec((1, LANES), row),
        pl.BlockSpec((1, LANES), row),
        pl.BlockSpec((1, LANES), row),
        pl.BlockSpec((1, LANES), row),
        pl.BlockSpec((VH_A, 1), row),
    ] + [page_spec(u) for u in range(g_pages)] * 2
    grid_spec = pltpu.PrefetchScalarGridSpec(
        num_scalar_prefetch=1,
        grid=(n_db, n_steps),
        in_specs=in_specs,
        out_specs=pl.BlockSpec((None, VH_A, LANES), lambda d, j, pt: (d, 0, 0)),
        scratch_shapes=[pltpu.VMEM((1, LANES), F32), pltpu.VMEM((1, LANES), F32),
                        pltpu.VMEM((VH_A, LANES), F32)],
    )
    out = pl.pallas_call(
        functools.partial(_dec_kernel, g_pages=g_pages, n_steps=n_steps, past=past, n_new=n_new),
        grid_spec=grid_spec,
        out_shape=jax.ShapeDtypeStruct((n_db, VH_A, LANES), F32),
        compiler_params=_params(("arbitrary", "arbitrary")),
        name="decode_attention",
    )(page_table.reshape(-1), sc, qcat, kn, vn, slc, tcol, hcol, map0, g_sub.reshape(VH_A, 1),
      *([cache_k] * g_pages), *([cache_v] * g_pages))
    out = out.reshape(n_db, VH_A, H_A, 2, n_new)[:, :, :, 0, :]
    return jnp.transpose(out, (0, 3, 2, 1)).reshape(n_sample, W_A)


def _gla_kernel(q_ref, k_ref, v_ref, la_ref, r_ref, s0_ref, gg_ref, o_ref, so_ref, state, *,
                chunk, sub, n_chunks, hi):
    state[...] = s0_ref[...]
    row = lax.broadcasted_iota(jnp.int32, (chunk, chunk), 0)
    col = lax.broadcasted_iota(jnp.int32, (chunk, chunk), 1)
    tri = (col <= row).astype(F32)
    tri_sub = (col <= (row // sub) * sub + (sub - 1)).astype(F32)
    causal = col <= row
    colblk = col // sub
    rowc = lax.broadcasted_iota(jnp.int32, (chunk, DK_B), 0)
    ones = jnp.ones((chunk, DK_B), F32)
    n_sub = chunk // sub

    def body(c, carry):
        c0 = pl.multiple_of(c * chunk, chunk)
        sl = pl.ds(c0, chunk)
        la = la_ref[sl, :]
        b = jnp.dot(tri, la, precision=HI, preferred_element_type=F32)
        rsub = jnp.dot(tri_sub, la, precision=HI, preferred_element_type=F32)
        b_end = b[chunk - 1:chunk, :]
        qs = q_ref[sl, :] * (DK_B ** -0.5)
        kk = k_ref[sl, :]
        vv = v_ref[sl, :]
        s_old = state[...]

        inter = _mm(qs * jnp.exp(b), s_old, hi)

        kt = kk * jnp.exp(rsub - b)
        qj = []
        for jb in range(n_sub):
            ref = rsub[jb * sub:jb * sub + 1, :]
            qj.append(qs * jnp.exp(jnp.where(rowc >= jb * sub, b - ref, -jnp.inf)))
        a4 = _mm_nt(jnp.concatenate(qj, axis=0), kt, hi)
        att = a4[:chunk]
        for jb in range(1, n_sub):
            att = jnp.where(colblk == jb, a4[jb * chunk:(jb + 1) * chunk], att)
        att = jnp.where(causal, att, 0.0)
        o = inter + _mm(att, vv, hi)

        dec = jnp.exp(_mm_tn(la, ones, True))
        kd = kk * jnp.exp(b_end - b)
        state[...] = jnp.concatenate([dec] * (DV_B // DK_B), axis=1) * s_old + _mm_tn(kd, vv, hi)

        ms = jnp.mean(o * o, axis=-1, keepdims=True)
        r = r_ref[sl, :]
        y = o * lax.rsqrt(ms + EPS) * gg_ref[...] * (r * jax.nn.sigmoid(r))
        o_ref[sl, :] = y.astype(o_ref.dtype)
        return carry

    lax.fori_loop(0, n_chunks, body, 0)
    so_ref[...] = state[...]


def _gla(z2, la, s0, g_gla, n_batch, seq, row0, chunk, sub, hi, out_dtype, name):
    rb0 = row0 // seq
    qc0 = W_A // DK_B
    kc0 = (W_A + W_KB) // DK_B
    vc0 = (W_A + 2 * W_KB) // DV_B
    rc0 = (W_A + 2 * W_KB + W_B) // DV_B
    per_batch_state = s0.shape[0] != 1
    return pl.pallas_call(
        functools.partial(_gla_kernel, chunk=chunk, sub=sub, n_chunks=seq // chunk, hi=hi),
        grid=(n_batch, H_B),
        in_specs=[
            pl.BlockSpec((seq, DK_B), lambda b, h: (rb0 + b, qc0 + h)),
            pl.BlockSpec((seq, DK_B), lambda b, h: (rb0 + b, kc0 + h)),
            pl.BlockSpec((seq, DV_B), lambda b, h: (rb0 + b, vc0 + h)),
            pl.BlockSpec((seq, DK_B), lambda b, h: (rb0 + b, h)),
            pl.BlockSpec((seq, DV_B), lambda b, h: (rb0 + b, rc0 + h)),
            pl.BlockSpec((None, None, DK_B, DV_B),
                         (lambda b, h: (b, h, 0, 0)) if per_batch_state else (lambda b, h: (0, h, 0, 0))),
            pl.BlockSpec((1, DV_B), lambda b, h: (0, 0)),
        ],
        out_specs=[
            pl.BlockSpec((seq, DV_B), lambda b, h: (b, h)),
            pl.BlockSpec((None, None, DK_B, DV_B), lambda b, h: (b, h, 0, 0)),
        ],
        out_shape=[jax.ShapeDtypeStruct((n_batch * seq, W_B), out_dtype),
                   jax.ShapeDtypeStruct((n_batch, H_B, DK_B, DV_B), F32)],
        scratch_shapes=[pltpu.VMEM((DK_B, DV_B), F32)],
        compiler_params=_params(("arbitrary", "arbitrary")),
        name=name,
    )(z2, z2, z2, la, z2, s0, g_gla.reshape(1, DV_B))


def _merge_kernel(oa_ref, og_ref, h_ref, wa_ref, wb_ref, wga_ref, wgb_ref, o_ref):
    h = h_ref[...]
    pa = jnp.dot(oa_ref[...], wa_ref[...], preferred_element_type=F32)
    ga = jax.nn.sigmoid(jnp.dot(h, wga_ref[...], preferred_element_type=F32))
    acc = ga * pa
    pb = jnp.dot(og_ref[...], wb_ref[...], preferred_element_type=F32)
    gb = jax.nn.sigmoid(jnp.dot(h, wgb_ref[...], preferred_element_type=F32))
    o_ref[...] = (acc + gb * pb).astype(o_ref.dtype)


def _merge(oa, og, h, wa, wb, wg, tm, tn):
    m, k = h.shape
    d = wa.shape[1]
    gb0 = d // tn
    return pl.pallas_call(
        _merge_kernel,
        grid=(d // tn, m // tm),
        in_specs=[
            pl.BlockSpec((tm, W_A), lambda j, i: (i, 0)),
            pl.BlockSpec((tm, W_B), lambda j, i: (i, 0)),
            pl.BlockSpec((tm, k), lambda j, i: (i, 0)),
            pl.BlockSpec((W_A, tn), lambda j, i: (0, j)),
            pl.BlockSpec((W_B, tn), lambda j, i: (0, j)),
            pl.BlockSpec((k, tn), lambda j, i: (0, j)),
            pl.BlockSpec((k, tn), lambda j, i: (0, gb0 + j)),
        ],
        out_specs=pl.BlockSpec((tm, tn), lambda j, i: (i, j)),
        out_shape=jax.ShapeDtypeStruct((m, d), BF16),
        compiler_params=_params(("arbitrary", "arbitrary")),
        name="merge",
    )(oa, og, h, wa, wb, wg, wg)


def _merge_s_kernel(oa_ref, og_ref, wa_ref, wb_ref, ga_ref, gb_ref, o_ref):
    pa = _mm(oa_ref[...], wa_ref[...], True)
    pb = _mm(og_ref[...], wb_ref[...], True)
    o_ref[...] = ga_ref[...] * pa + gb_ref[...] * pb


def _merge_sample(oa, og, wa, wb, sg, tn):
    m = oa.shape[0]
    d = wa.shape[1]
    gb0 = d // tn
    return pl.pallas_call(
        _merge_s_kernel,
        grid=(d // tn,),
        in_specs=[
            pl.BlockSpec((m, W_A), lambda j: (0, 0)),
            pl.BlockSpec((m, W_B), lambda j: (0, 0)),
            pl.BlockSpec((W_A, tn), lambda j: (0, j)),
            pl.BlockSpec((W_B, tn), lambda j: (0, j)),
            pl.BlockSpec((m, tn), lambda j: (0, j)),
            pl.BlockSpec((m, tn), lambda j: (0, gb0 + j)),
        ],
        out_specs=pl.BlockSpec((m, tn), lambda j: (0, j)),
        out_shape=jax.ShapeDtypeStruct((m, d), F32),
        compiler_params=_params(("arbitrary",)),
        name="merge_sample",
    )(oa, og, wa, wb, sg, sg)


def _outproj_kernel(mg_ref, w_ref, x_ref, gf_ref, x1_ref, h2_ref, *, hi):
    x1 = x_ref[...] + _mm(mg_ref[...], w_ref[...], hi)
    x1_ref[...] = x1
    ms = jnp.mean(x1 * x1, axis=-1, keepdims=True)
    h2_ref[...] = (x1 * lax.rsqrt(ms + EPS) * gf_ref[...]).astype(h2_ref.dtype)


def _outproj(mg, w, x, g_ffn, tm, hi, name):
    m, d = x.shape
    return pl.pallas_call(
        functools.partial(_outproj_kernel, hi=hi),
        grid=(m // tm,),
        in_specs=[
            pl.BlockSpec((tm, d), lambda i: (i, 0)),
            pl.BlockSpec((d, d), lambda i: (0, 0)),
            pl.BlockSpec((tm, d), lambda i: (i, 0)),
            pl.BlockSpec((1, d), lambda i: (0, 0)),
        ],
        out_specs=[pl.BlockSpec((tm, d), lambda i: (i, 0)), pl.BlockSpec((tm, d), lambda i: (i, 0))],
        out_shape=[jax.ShapeDtypeStruct((m, d), F32), jax.ShapeDtypeStruct((m, d), F32)],
        compiler_params=_params(("arbitrary",)),
        name=name,
    )(mg, w, x, g_ffn.reshape(1, d))


def _route_kernel(h_ref, w_ref, b_ref, c0_ref, idx_ref, gate_ref, rank_ref, cnt_ref, carry, *, tm, hi):
    @pl.when(pl.program_id(0) == 0)
    def _init():
        carry[...] = c0_ref[...]

    logits = _mm(h_ref[...], w_ref[...], hi) + b_ref[...]
    lane = lax.broadcasted_iota(jnp.int32, (tm, N_EXPERTS), 1).astype(F32)
    work = logits
    vals, idxs, sels = [], [], []
    for _ in range(TOP_K):
        mx = jnp.max(work, axis=-1, keepdims=True)
        ix = jnp.min(jnp.where(work == mx, lane, float(N_EXPERTS)), axis=-1, keepdims=True)
        sel = lane == ix
        vals.append(mx)
        idxs.append(ix)
        sels.append(sel)
        work = jnp.where(sel, -jnp.inf, work)
    es = [jnp.exp(v - vals[0]) for v in vals]
    tot = es[0] + es[1] + es[2] + es[3]
    gate_ref[...] = jnp.concatenate([e / tot for e in es], axis=1)
    idx_ref[...] = jnp.concatenate(idxs, axis=1).astype(jnp.int32)

    onehot = jnp.zeros((tm, N_EXPERTS), F32)
    for sel in sels:
        onehot = onehot + sel.astype(F32)
    r = lax.broadcasted_iota(jnp.int32, (tm, tm), 0)
    c = lax.broadcasted_iota(jnp.int32, (tm, tm), 1)
    before = (c < r).astype(BF16)
    cum = jnp.dot(before, onehot.astype(BF16), preferred_element_type=F32) + carry[...]
    ranks = [jnp.sum(jnp.where(sel, cum, 0.0), axis=-1, keepdims=True) for sel in sels]
    rank_ref[...] = jnp.concatenate(ranks, axis=1).astype(jnp.int32)
    carry[...] = carry[...] + jnp.sum(onehot, axis=0, keepdims=True)
    cnt_ref[...] = carry[...]


def _route(h, n_tok, w_router, b_router, c0, tm, hi, name):
    d = h.shape[1]
    assert n_tok % tm == 0
    return pl.pallas_call(
        functools.partial(_route_kernel, tm=tm, hi=hi),
        grid=(n_tok // tm,),
        in_specs=[
            pl.BlockSpec((tm, d), lambda i: (i, 0)),
            pl.BlockSpec((d, N_EXPERTS), lambda i: (0, 0)),
            pl.BlockSpec((1, N_EXPERTS), lambda i: (0, 0)),
            pl.BlockSpec((1, N_EXPERTS), lambda i: (0, 0)),
        ],
        out_specs=[
            pl.BlockSpec((tm, TOP_K), lambda i: (i, 0)),
            pl.BlockSpec((tm, TOP_K), lambda i: (i, 0)),
            pl.BlockSpec((tm, TOP_K), lambda i: (i, 0)),
            pl.BlockSpec((1, N_EXPERTS), lambda i: (0, 0)),
        ],
        out_shape=[
            jax.ShapeDtypeStruct((n_tok, TOP_K), jnp.int32),
            jax.ShapeDtypeStruct((n_tok, TOP_K), F32),
            jax.ShapeDtypeStruct((n_tok, TOP_K), jnp.int32),
            jax.ShapeDtypeStruct((1, N_EXPERTS), F32),
        ],
        scratch_shapes=[pltpu.VMEM((1, N_EXPERTS), F32)],
        compiler_params=_params(("arbitrary",)),
        name=name,
    )(h, w_router, b_router.reshape(1, N_EXPERTS), c0)


def _up_kernel(be_ref, first_ref, nu_ref, x_ref, wg_ref, wl_ref, bg_ref, bl_ref, o_ref, wgb, wlb):
    b = pl.program_id(1)

    @pl.when(first_ref[b] == 1)
    def _cast():
        wgb[...] = wg_ref[...].astype(BF16)
        wlb[...] = wl_ref[...].astype(BF16)

    @pl.when(b < nu_ref[0])
    def _compute():
        x = x_ref[...]
        g = jnp.dot(x, wgb[...], preferred_element_type=F32) + bg_ref[...]
        lin = jnp.dot(x, wlb[...], preferred_element_type=F32) + bl_ref[...]
        g = jnp.minimum(g, SWIGLU_LIMIT)
        lin = jnp.clip(lin, -SWIGLU_LIMIT, SWIGLU_LIMIT)
        glu = g * jax.nn.sigmoid(SWIGLU_ALPHA * g)
        o_ref[...] = ((lin + 1.0) * glu).astype(o_ref.dtype)

    @pl.when(b >= nu_ref[0])
    def _unused():
        o_ref[...] = jnp.zeros(o_ref.shape, o_ref.dtype)


def _expert_up(be, first, nu, xbuf, w_up, b_up, tm, th):
    rows, d = xbuf.shape
    d_e = w_up.shape[-1] // 2
    nj = d_e // th
    grid_spec = pltpu.PrefetchScalarGridSpec(
        num_scalar_prefetch=3,
        grid=(nj, rows // tm),
        in_specs=[
            pl.BlockSpec((tm, d), lambda j, b, be, fi, nu: (b, 0)),
            pl.BlockSpec((None, None, d, th), lambda j, b, be, fi, nu: (0, be[b], 0, j)),
            pl.BlockSpec((None, None, d, th), lambda j, b, be, fi, nu: (0, be[b], 0, nj + j)),
            pl.BlockSpec((None, None, 1, th), lambda j, b, be, fi, nu: (0, be[b], 0, j)),
            pl.BlockSpec((None, None, 1, th), lambda j, b, be, fi, nu: (0, be[b], 0, nj + j)),
        ],
        out_specs=pl.BlockSpec((tm, th), lambda j, b, be, fi, nu: (b, j)),
        scratch_shapes=[pltpu.VMEM((d, th), BF16), pltpu.VMEM((d, th), BF16)],
    )
    return pl.pallas_call(
        _up_kernel,
        grid_spec=grid_spec,
        out_shape=jax.ShapeDtypeStruct((rows, d_e), BF16),
        compiler_params=_params(("arbitrary", "arbitrary")),
        name="expert_up",
    )(be, first, nu, xbuf, w_up, w_up, b_up, b_up)


def _down_kernel(be_ref, first_ref, nu_ref, a_ref, w_ref, b_ref, o_ref, wb):
    b = pl.program_id(1)

    @pl.when(first_ref[b] == 1)
    def _cast():
        wb[...] = w_ref[...].astype(BF16)

    @pl.when(b < nu_ref[0])
    def _compute():
        o_ref[...] = jnp.dot(a_ref[...], wb[...], preferred_element_type=F32) + b_ref[...]

    @pl.when(b >= nu_ref[0])
    def _unused():
        o_ref[...] = jnp.zeros(o_ref.shape, o_ref.dtype)


def _expert_down(be, first, nu, act, w_down, b_down, tm, tn):
    rows, d_e = act.shape
    d = w_down.shape[-1]
    grid_spec = pltpu.PrefetchScalarGridSpec(
        num_scalar_prefetch=3,
        grid=(d // tn, rows // tm),
        in_specs=[
            pl.BlockSpec((tm, d_e), lambda j, b, be, fi, nu: (b, 0)),
            pl.BlockSpec((None, None, d_e, tn), lambda j, b, be, fi, nu: (0, be[b], 0, j)),
            pl.BlockSpec((None, None, 1, tn), lambda j, b, be, fi, nu: (0, be[b], 0, j)),
        ],
        out_specs=pl.BlockSpec((tm, tn), lambda j, b, be, fi, nu: (b, j)),
        scratch_shapes=[pltpu.VMEM((d_e, tn), BF16)],
    )
    return pl.pallas_call(
        _down_kernel,
        grid_spec=grid_spec,
        out_shape=jax.ShapeDtypeStruct((rows, d), F32),
        compiler_params=_params(("arbitrary", "arbitrary")),
        name="expert_down",
    )(be, first, nu, act, w_down, b_down)


def _row_copy(src_hbm, row, dst, sem):
    return pltpu.make_async_copy(src_hbm.at[pl.ds(row, 1)], dst, sem)


def _dispatch_kernel(tok_ref, h_hbm, o_ref, buf, sem, *, tm):
    b = pl.program_id(0)

    def start_tile(blk, slot):
        def body(r2, carry):
            for p in range(2):
                r = 2 * r2 + p
                _row_copy(h_hbm, tok_ref[blk * tm + r], buf.at[slot, pl.ds(r, 1)], sem.at[slot]).start(priority=p)
            return carry
        lax.fori_loop(0, tm // 2, body, 0, unroll=4)

    @pl.when(b == 0)
    def _first():
        start_tile(0, 0)

    @pl.when(b + 1 < pl.num_programs(0))
    def _next():
        start_tile(b + 1, (b + 1) % 2)

    slot = b % 2

    def wait_body(r, carry):
        _row_copy(h_hbm, 0, buf.at[slot, pl.ds(r, 1)], sem.at[slot]).wait()
        return carry
    lax.fori_loop(0, tm, wait_body, 0, unroll=8)
    o_ref[...] = buf[slot].astype(o_ref.dtype)


def _dispatch(tok_of_row, h, tm):
    rows = tok_of_row.shape[0]
    d = h.shape[1]
    grid_spec = pltpu.PrefetchScalarGridSpec(
        num_scalar_prefetch=1,
        grid=(rows // tm,),
        in_specs=[pl.BlockSpec(memory_space=pl.ANY)],
        out_specs=pl.BlockSpec((tm, d), lambda b, tok: (b, 0)),
        scratch_shapes=[pltpu.VMEM((2, tm, d), F32), pltpu.SemaphoreType.DMA((2,))],
    )
    return pl.pallas_call(
        functools.partial(_dispatch_kernel, tm=tm),
        grid_spec=grid_spec,
        out_shape=jax.ShapeDtypeStruct((rows, d), BF16),
        compiler_params=_params(("arbitrary",)),
        name="moe_dispatch",
    )(tok_of_row, h)


def _combine_kernel(dest_ref, y_hbm, x_ref, gate_ref, o_ref, buf, sem, *, tm):
    i = pl.program_id(0)

    def start_tile(blk, slot):
        def body(r, carry):
            for k in range(TOP_K):
                row = dest_ref[(blk * tm + r) * TOP_K + k]
                _row_copy(y_hbm, row, buf.at[slot, k, pl.ds(r, 1)], sem.at[slot]).start(priority=k % 2)
            return carry
        lax.fori_loop(0, tm, body, 0, unroll=4)

    @pl.when(i == 0)
    def _first():
        start_tile(0, 0)

    @pl.when(i + 1 < pl.num_programs(0))
    def _next():
        start_tile(i + 1, (i + 1) % 2)

    slot = i % 2

    def wait_body(r, carry):
        for k in range(TOP_K):
            _row_copy(y_hbm, 0, buf.at[slot, k, pl.ds(r, 1)], sem.at[slot]).wait()
        return carry
    lax.fori_loop(0, tm, wait_body, 0, unroll=4)
    g = gate_ref[...]
    y = g[:, 0:1] * buf[slot, 0]
    for k in range(1, TOP_K):
        y = y + g[:, k:k + 1] * buf[slot, k]
    o_ref[...] = x_ref[...] + y


def _combine(dest, ybuf, x1, gate, tm):
    n_tok, d = x1.shape
    assert n_tok % tm == 0
    grid_spec = pltpu.PrefetchScalarGridSpec(
        num_scalar_prefetch=1,
        grid=(n_tok // tm,),
        in_specs=[pl.BlockSpec(memory_space=pl.ANY),
                  pl.BlockSpec((tm, d), lambda i, dest: (i, 0)),
                  pl.BlockSpec((tm, TOP_K), lambda i, dest: (i, 0))],
        out_specs=pl.BlockSpec((tm, d), lambda i, dest: (i, 0)),
        scratch_shapes=[pltpu.VMEM((2, TOP_K, tm, d), F32), pltpu.SemaphoreType.DMA((2,))],
    )
    return pl.pallas_call(
        functools.partial(_combine_kernel, tm=tm),
        grid_spec=grid_spec,
        out_shape=jax.ShapeDtypeStruct((n_tok, d), F32),
        compiler_params=_params(("arbitrary",)),
        name="moe_combine",
    )(dest.reshape(-1), ybuf, x1, gate)


def kernel(x_prompt, x_sample, cache_k, cache_v, state_gla, page_table, meta_tokens, g_mix, w_in, g_q, g_k, lam_q1, lam_k1, lam_q2, lam_k2, g_sub, w_fg, b_fg, g_gla, w_br_a, w_br_b, w_out, g_ffn, w_router, b_router, w_e_up, b_e_up, w_e_down, b_e_down):
    assert w_in.shape[0] == 1, "single-layer trunk"
    n_batch, seq, d = x_prompt.shape
    n_db, n_new, _ = x_sample.shape
    assert 2 * H_A * n_new == LANES
    n_prompt = n_batch * seq
    n_sample = n_db * n_new
    n_tok = n_prompt + n_sample
    row_m = n_prompt

    def layer0(x):
        return x.reshape(x.shape[1:])

    w = layer0(w_in)
    w_a, w_b, w_o, w_r = layer0(w_br_a), layer0(w_br_b), layer0(w_out), layer0(w_router)
    lam =(jnp.exp(jnp.sum(lam_q1[0] * lam_k1[0])) - jnp.exp(jnp.sum(lam_q2[0] * lam_k2[0])) + LAM_INIT)
    slopes = jnp.exp2(-8.0 * jnp.arange(1, H_A + 1, dtype=F32) / H_A)
    sc = jnp.concatenate([lam.reshape(1), slopes]).astype(F32)
    g_qk = jnp.stack([g_q[0].reshape(1, VH_A), g_k[0].reshape(1, VH_A)])
    wfg_pad = jnp.zeros((LANES, W_KB), F32).at[:GLA_RANK].set(w_fg[0])
    bfg = b_fg[0].reshape(1, W_KB)
    w_gates = w[:, COL_G:]

    pad_rows = (-(n_prompt + N_META)) % (5 * 13 * LANES)
    xm = jnp.concatenate([x_prompt.reshape(n_prompt, d), meta_tokens.astype(F32),
                          jnp.zeros((pad_rows, d), F32)], axis=0)
    n_rows = xm.shape[0]
    tm_proj = n_rows // 5
    tm_row = n_rows // 13
    hm = _rms(xm, g_mix[0], BF16, tm=tm_row)
    qk = _proj(hm, w_in, COL_QK, 2 * W_A, 512, tm_proj, "qk", False, (g_qk,), name="proj_qk")
    z2 = _proj(hm, w_in, COL_Z2, Z2_COLS, 512, tm_proj, "plain", False, name="proj_z2")
    la = _proj(hm, w_in, COL_A, LANES, LANES, tm_proj, "loga", False, (wfg_pad, bfg), name="proj_loga")

    oa_p = _prompt_attention(sc, qk, z2, g_sub[0], n_batch, seq, row_m)
    zero_state = jnp.zeros((1, H_B, DK_B, DV_B), F32)
    _, s_meta = _gla(z2, la, zero_state, g_gla[0], 1, N_META, row_m, N_META, N_META, False, BF16, "gla_meta")
    og_p, s_prompt = _gla(z2, la, s_meta, g_gla[0], n_batch, seq, 0, GLA_CHUNK, GLA_SUB, False, BF16,
                          "gla_prompt")
    tail = jnp.zeros((n_rows - n_prompt, W_A), BF16)
    mg = _merge(jnp.concatenate([oa_p, tail], axis=0), jnp.concatenate([og_p, tail], axis=0), hm,
                w_a.astype(BF16), w_b.astype(BF16), w_gates.astype(BF16), tm_row, 512)
    x1_m, h2_m = _outproj(mg, w_o.astype(BF16), xm, g_ffn[0], tm_row // 2, False, "outproj")

    xs = x_sample.reshape(n_sample, d)
    hs = _rms(xs, g_mix[0], F32, tm=n_sample)
    qk_s = _proj(hs, w_in, COL_QK, 2 * W_A, 512, n_sample, "qk", True, (g_qk,), name="proj_qk_s")
    z2_s = _proj(hs, w_in, COL_Z2, Z2_COLS, 512, n_sample, "plain", True, name="proj_z2_s")
    la_s = _proj(hs, w_in, COL_A, LANES, LANES, n_sample, "loga", True, (wfg_pad, bfg), name="proj_loga_s")
    sg_s = _proj(hs, w_gates, 0, 2 * d, 512, n_sample, "gate", True, name="proj_gate_s")

    oa_s = _decode_attention(page_table, sc, qk_s, z2_s, cache_k, cache_v, slopes, g_sub[0])
    og_s, s_sample = _gla(z2_s, la_s, layer0(state_gla), g_gla[0], n_db, n_new, 0, n_new, n_new, True, F32,
                          "gla_sample")
    mg_s = _merge_sample(oa_s, og_s, w_a, w_b, sg_s, 512)
    x1_s, h2_s = _outproj(mg_s, w_o, xs, g_ffn[0], n_sample, True, "outproj_sample")

    c0 = jnp.zeros((1, N_EXPERTS), F32)
    idx_m, gate_m, rank_m, cnt_m = _route(h2_m, n_prompt, w_r, b_router[0], c0, 512, False, "route_prompt")
    idx_s, gate_s, rank_s, cnt = _route(h2_s, n_sample, w_r, b_router[0], cnt_m, n_sample, True,
                                        "route_sample")
    eidx = jnp.concatenate([idx_m, idx_s], axis=0)
    gate = jnp.concatenate([gate_m, gate_s], axis=0)
    rank = jnp.concatenate([rank_m, rank_s], axis=0)
    h2 = jnp.concatenate([h2_m[:n_prompt], h2_s], axis=0)
    x1 = jnp.concatenate([x1_m[:n_prompt], x1_s], axis=0)

    tme = 256
    n_blocks = -(-(n_tok * TOP_K) // tme) + N_EXPERTS
    counts = cnt.reshape(N_EXPERTS).astype(jnp.int32)
    padded = (counts + tme - 1) // tme * tme
    pend = jnp.cumsum(padded)
    pstart = pend - padded
    dest = pstart[eidx] + rank
    n_used = pend[-1] // tme
    blk = jnp.arange(n_blocks, dtype=jnp.int32)
    be = jnp.minimum(jnp.sum(pend[None, :] <= (blk * tme)[:, None], axis=1), N_EXPERTS - 1).astype(jnp.int32)
    be = jnp.where(blk < n_used, be, be[jnp.maximum(n_used - 1, 0)])
    first = jnp.concatenate([jnp.ones((1,), jnp.int32), (be[1:] != be[:-1]).astype(jnp.int32)])
    nu = n_used.reshape(1).astype(jnp.int32)
    tok = jnp.repeat(jnp.arange(n_tok, dtype=jnp.int32), TOP_K)
    tok_of_row = (jnp.arange(n_blocks * tme, dtype=jnp.int32) % n_tok).at[dest.reshape(-1)].set(tok)

    xbuf = _dispatch(tok_of_row, h2, tme)
    act = _expert_up(be, first, nu, xbuf, w_e_up, b_e_up.reshape(1, N_EXPERTS, 1, -1), tme, 1024)
    ybuf = _expert_down(be, first, nu, act, w_e_down, b_e_down.reshape(1, N_EXPERTS, 1, -1), tme, 1024)
    out = _combine(dest, ybuf, x1, gate, 64)

    y_prompt = out[:n_prompt].reshape(n_batch, seq, d)
    y_sample = out[n_prompt:].reshape(n_db, n_new, d)
    k_meta = jnp.broadcast_to(qk[row_m:row_m + N_META, W_A:][None], (n_batch, N_META, W_A))
    v_meta = jnp.broadcast_to(z2[row_m:row_m + N_META, :W_A][None], (n_batch, N_META, W_A))
    k_prompt = jnp.concatenate([k_meta, qk[:n_prompt, W_A:].reshape(n_batch, seq, W_A)], axis=1)
    v_prompt = jnp.concatenate([v_meta, z2[:n_prompt, :W_A].reshape(n_batch, seq, W_A)], axis=1)
    k_prompt = k_prompt.reshape(1, n_batch, seq + N_META, H_A, VH_A)
    v_prompt = v_prompt.reshape(1, n_batch, seq + N_META, H_A, VH_A)
    k_sample = qk_s[:, W_A:].reshape(1, n_db, n_new, H_A, VH_A)
    v_sample = z2_s[:, :W_A].reshape(1, n_db, n_new, H_A, VH_A)
    return (y_prompt, y_sample, k_prompt, v_prompt, s_prompt[None], k_sample, v_sample, s_sample[None])
```
